```python
import math
import jax, jax.numpy as jnp
from jax import lax
import numpy as np

D_MODEL = 2048
BATCH = 2
SEQ = 8192
DEPTH = 1

MIX_WIDTH = D_MODEL
GDN_WIDTH = MIX_WIDTH // 2
GDN_HEAD_DIM = 128
GDN_HEADS = GDN_WIDTH // GDN_HEAD_DIM
GDN_CHUNK = 64
CONV_WIDTH = 4
RG_WIDTH = MIX_WIDTH - GDN_WIDTH
RG_BLOCKS = 8
RG_BLOCK_DIM = RG_WIDTH // RG_BLOCKS
LRU_C = 8.0
N_GROUPS = 4
EXPERTS_PER_GROUP = 8
N_EXPERTS = N_GROUPS * EXPERTS_PER_GROUP
TOP_K = 2
EXPERT_FF = D_MODEL // 4
NORM_EPS = 1e-6
IN_COLS = 4 * GDN_WIDTH + 2 * GDN_HEADS + 2 * RG_WIDTH

kernel_name = "hymba_gdn_rglru_hmoe_block"


def rms_norm(x, w):
    xf = x.astype(jnp.float32)
    y = xf * lax.rsqrt(jnp.mean(xf * xf, axis=-1, keepdims=True) + NORM_EPS)
    return (y * w.astype(jnp.float32)).astype(x.dtype)


def l2_normalize(x):
    return x * lax.rsqrt(jnp.sum(x * x, axis=-1, keepdims=True) + NORM_EPS)


def causal_depthwise_conv(x, w):
    c = x.shape[-1]
    return lax.conv_general_dilated(
        x, w[:, None, :].astype(x.dtype), window_strides=(1,),
        padding=[(w.shape[0] - 1, 0)], dimension_numbers=("NWC", "WIO", "NWC"),
        feature_group_count=c)


def chunk_gated_delta_rule(q, k, v, g, beta):
    b_, t_, h_, dk = q.shape
    dv = v.shape[-1]
    c_ = GDN_CHUNK
    n_ = t_ // c_

    def to_chunks(t):
        return t.reshape(b_, n_, c_, h_, -1).transpose(0, 3, 1, 2, 4)

    q, k, v = to_chunks(q), to_chunks(k), to_chunks(v)
    g = g.reshape(b_, n_, c_, h_).transpose(0, 3, 1, 2)
    beta = beta.reshape(b_, n_, c_, h_).transpose(0, 3, 1, 2)
    g_cum = jnp.cumsum(g, axis=-1)

    causal = jnp.tril(jnp.ones((c_, c_), dtype=bool))
    strict = jnp.tril(jnp.ones((c_, c_), dtype=bool), -1)
    diff = g_cum[..., :, None] - g_cum[..., None, :]
    decay = jnp.exp(jnp.where(causal, diff, -jnp.inf))

    kk = jnp.einsum("bhnik,bhnjk->bhnij", k, k)
    a_strict = jnp.where(strict, beta[..., :, None] * kk * decay, 0.0)
    lhs = a_strict + jnp.eye(c_, dtype=jnp.float32)
    rhs = jnp.concatenate(
        [v * beta[..., None], k * (beta * jnp.exp(g_cum))[..., None]], axis=-1)
    sol = lax.linalg.triangular_solve(lhs, rhs, left_side=True, lower=True,
                                      unit_diagonal=True)
    u = sol[..., :dv]
    w = sol[..., dv:]

    attn_intra = jnp.einsum("bhnik,bhnjk->bhnij", q, k) * decay

    def move(t):
        return jnp.moveaxis(t, 2, 0)

    def step(state, inp):
        q_c, k_c, u_c, w_c, g_c, at_c = inp
        v_new = u_c - jnp.einsum("bhck,bhkv->bhcv", w_c, state)
        o_c = (jnp.einsum("bhck,bhkv->bhcv", q_c * jnp.exp(g_c)[..., None], state)
               + jnp.einsum("bhij,bhjv->bhiv", at_c, v_new))
        g_last = g_c[..., -1]
        k_dec = k_c * jnp.exp(g_last[..., None] - g_c)[..., None]
        state = (state * jnp.exp(g_last)[..., None, None]
                 + jnp.einsum("bhck,bhcv->bhkv", k_dec, v_new))
        return state, o_c

    s0 = jnp.zeros((b_, h_, dk, dv), jnp.float32)
    _, o = lax.scan(step, s0, (move(q), move(k), move(u), move(w), move(g_cum),
                               move(attn_intra)))
    return o.transpose(1, 0, 3, 2, 4).reshape(b_, t_, h_, dv)


def gated_deltanet_group(q, k, v, z, b, a, conv_w, a_log, dt_bias, norm_w):
    b_, t_, _ = q.shape
    qkv = jax.nn.silu(causal_depthwise_conv(jnp.concatenate([q, k, v], axis=-1), conv_w))
    qkv = qkv.astype(jnp.float32)
    q, k, v = jnp.split(qkv, 3, axis=-1)
    shp = (b_, t_, GDN_HEADS, GDN_HEAD_DIM)
    q = l2_normalize(q.reshape(shp)) * (GDN_HEAD_DIM ** -0.5)
    k = l2_normalize(k.reshape(shp))
    v = v.reshape(shp)
    beta = jax.nn.sigmoid(b.astype(jnp.float32))
    g = -jnp.exp(a_log.astype(jnp.float32)) * jax.nn.softplus(
        a.astype(jnp.float32) + dt_bias.astype(jnp.float32))
    o = chunk_gated_delta_rule(q, k, v, g, beta)
    o = rms_norm(o, norm_w) * jax.nn.silu(z.astype(jnp.float32).reshape(shp))
    return o.reshape(b_, t_, GDN_WIDTH)


def lru_combine(c1, c2):
    a1, b1 = c1
    a2, b2 = c2
    return a1 * a2, a2 * b1 + b2


def rglru_group(xr, gate, conv_w, conv_b, w_rgate, b_rgate, w_igate, b_igate, lam, norm_w):
    b_, t_, _ = xr.shape
    xc = (causal_depthwise_conv(xr, conv_w) + conv_b).astype(jnp.float32)
    xb = xc.reshape(b_, t_, RG_BLOCKS, RG_BLOCK_DIM)
    r = jax.nn.sigmoid(jnp.einsum("btni,nij->btnj", xb, w_rgate.astype(jnp.float32))
                       + b_rgate.astype(jnp.float32).reshape(RG_BLOCKS, RG_BLOCK_DIM))
    i = jax.nn.sigmoid(jnp.einsum("btni,nij->btnj", xb, w_igate.astype(jnp.float32))
                       + b_igate.astype(jnp.float32).reshape(RG_BLOCKS, RG_BLOCK_DIM))
    r = r.reshape(b_, t_, RG_WIDTH)
    i = i.reshape(b_, t_, RG_WIDTH)
    log_a = -LRU_C * r * jax.nn.softplus(-lam.astype(jnp.float32))
    a_t = jnp.exp(log_a)
    mult = jnp.sqrt(-jnp.expm1(2.0 * log_a))
    _, h = lax.associative_scan(lru_combine, (a_t, mult * (i * xc)), axis=1)
    y = h * jax.nn.gelu(gate.astype(jnp.float32))
    y = rms_norm(y.reshape(b_, t_, RG_BLOCKS, RG_BLOCK_DIM),
                 norm_w.reshape(RG_BLOCKS, RG_BLOCK_DIM))
    return y.reshape(b_, t_, RG_WIDTH)


def hierarchical_moe(h, w_group, b_group, w_expert, b_expert, w1, w3, w2):
    b_, t_, d_ = h.shape
    xt = h.reshape(-1, d_)
    grp_prob = jax.nn.softmax((xt @ w_group).astype(jnp.float32)
                              + b_group.astype(jnp.float32), axis=-1)
    grp_p, grp_idx = lax.top_k(grp_prob, 1)
    exp_logits = ((xt @ w_expert).astype(jnp.float32) + b_expert.astype(jnp.float32)
                  ).reshape(-1, N_GROUPS, EXPERTS_PER_GROUP)
    sel_logits = jnp.take_along_axis(exp_logits, grp_idx[:, :, None], axis=1)[:, 0]
    top_p, top_idx = lax.top_k(jax.nn.softmax(sel_logits, axis=-1), TOP_K)
    top_p = top_p / jnp.sum(top_p, axis=-1, keepdims=True)
    gate = grp_p * top_p
    global_idx = grp_idx * EXPERTS_PER_GROUP + top_idx
    combine = jnp.sum(jax.nn.one_hot(global_idx, N_EXPERTS, dtype=jnp.float32)
                      * gate[..., None], axis=1)
    y = jnp.zeros(xt.shape, jnp.float32)
    for e in range(N_EXPERTS):
        he = jax.nn.silu(xt @ w1[e]) * (xt @ w3[e])
        y = y + combine[:, e:e + 1] * (he @ w2[e]).astype(jnp.float32)
    return y.astype(h.dtype).reshape(b_, t_, d_)


def setup_inputs(seed: int = 0) -> dict:
    key = jax.random.key(seed)
    ks = jax.random.split(key, 25)
    L = DEPTH
    f32 = jnp.float32

    def nrm(k, shape, scale):
        return jax.random.normal(k, shape, f32) * scale

    def gain(k, shape):
        return 1.0 + 0.02 * jax.random.normal(k, shape, f32)

    dt = jnp.exp(jax.random.uniform(ks[5], (L, GDN_HEADS), f32,
                                    minval=math.log(1e-3), maxval=math.log(1e-1)))
    a_init = jax.random.uniform(ks[13], (L, RG_WIDTH), f32, minval=0.9, maxval=0.999)
    a_base = a_init ** (1.0 / LRU_C)
    return {
        "x": nrm(ks[0], (BATCH, SEQ, D_MODEL), 1.0),
        "norm_mix_w": gain(ks[1], (L, D_MODEL)),
        "w_in": nrm(ks[2], (L, D_MODEL, IN_COLS), D_MODEL ** -0.5),
        "gdn_conv_w": nrm(ks[3], (L, CONV_WIDTH, 3 * GDN_WIDTH), CONV_WIDTH ** -0.5),
        "gdn_a_log": jnp.log(jax.random.uniform(ks[4], (L, GDN_HEADS), f32,
                                                minval=1.0, maxval=16.0)),
        "gdn_dt_bias": dt + jnp.log(-jnp.expm1(-dt)),
        "gdn_norm_w": gain(ks[6], (L, GDN_HEAD_DIM)),
        "rg_conv_w": nrm(ks[7], (L, CONV_WIDTH, RG_WIDTH), CONV_WIDTH ** -0.5),
        "rg_conv_b": nrm(ks[8], (L, RG_WIDTH), 0.01),
        "rg_w_rgate": nrm(ks[9], (L, RG_BLOCKS, RG_BLOCK_DIM, RG_BLOCK_DIM), RG_BLOCK_DIM ** -0.5),
        "rg_b_rgate": nrm(ks[10], (L, RG_WIDTH), 0.01),
        "rg_w_igate": nrm(ks[11], (L, RG_BLOCKS, RG_BLOCK_DIM, RG_BLOCK_DIM), RG_BLOCK_DIM ** -0.5),
        "rg_b_igate": nrm(ks[12], (L, RG_WIDTH), 0.01),
        "rg_lambda": jnp.log(a_base) - jnp.log1p(-a_base),
        "rg_norm_w": gain(ks[14], (L, RG_WIDTH)),
        "w_out": nrm(ks[15], (L, MIX_WIDTH, D_MODEL), MIX_WIDTH ** -0.5),
        "norm_ffn_w": gain(ks[16], (L, D_MODEL)),
        "router_w_group": nrm(ks[17], (L, D_MODEL, N_GROUPS), D_MODEL ** -0.5),
        "router_b_group": nrm(ks[18], (L, N_GROUPS), 0.01),
        "router_w_expert": nrm(ks[19], (L, D_MODEL, N_EXPERTS), D_MODEL ** -0.5),
        "router_b_expert": nrm(ks[20], (L, N_EXPERTS), 0.01),
        "expert_w1": nrm(ks[21], (L, N_EXPERTS, D_MODEL, EXPERT_FF), D_MODEL ** -0.5),
        "expert_w3": nrm(ks[22], (L, N_EXPERTS, D_MODEL, EXPERT_FF), D_MODEL ** -0.5),
        "expert_w2": nrm(ks[23], (L, N_EXPERTS, EXPERT_FF, D_MODEL), EXPERT_FF ** -0.5),
        "norm_final_w": gain(ks[24], (D_MODEL,)),
    }


def reference(x, norm_mix_w, w_in, gdn_conv_w, gdn_a_log, gdn_dt_bias, gdn_norm_w,
              rg_conv_w, rg_conv_b, rg_w_rgate, rg_b_rgate, rg_w_igate, rg_b_igate,
              rg_lambda, rg_norm_w, w_out, norm_ffn_w, router_w_group, router_b_group,
              router_w_expert, router_b_expert, expert_w1, expert_w3, expert_w2,
              norm_final_w):
    widths = [GDN_WIDTH, GDN_WIDTH, GDN_WIDTH, GDN_WIDTH, GDN_HEADS, GDN_HEADS,
              RG_WIDTH, RG_WIDTH]
    split_points = [int(s) for s in np.cumsum(widths)[:-1]]
    for l in range(DEPTH):
        h = rms_norm(x, norm_mix_w[l])
        proj = h @ w_in[l]
        q, k, v, z, b, a, rx, rgate = jnp.split(proj, split_points, axis=-1)
        y_gdn = gated_deltanet_group(q, k, v, z, b, a, gdn_conv_w[l], gdn_a_log[l],
                                     gdn_dt_bias[l], gdn_norm_w[l])
        y_rg = rglru_group(rx, rgate, rg_conv_w[l], rg_conv_b[l], rg_w_rgate[l],
                           rg_b_rgate[l], rg_w_igate[l], rg_b_igate[l], rg_lambda[l],
                           rg_norm_w[l])
        mix = jnp.concatenate([y_gdn, y_rg], axis=-1).astype(x.dtype)
        x = x + mix @ w_out[l]
        h = rms_norm(x, norm_ffn_w[l])
        x = x + hierarchical_moe(h, router_w_group[l], router_b_group[l],
                                 router_w_expert[l], router_b_expert[l],
                                 expert_w1[l], expert_w3[l], expert_w2[l])
    return rms_norm(x, norm_final_w)
```

```python
import functools

import jax
import jax.numpy as jnp
from jax import lax
from jax.experimental import pallas as pl
from jax.experimental.pallas import tpu as pltpu

F32 = jnp.float32
BF16 = jnp.bfloat16
NORM_EPS = 1e-6
LRU_C = 8.0
LANES = 128
SUBLANES = 8
GDN_CHUNK = 64
HEAD_DIM = 128
CONV_W = 4
HALO = SUBLANES
VMEM_LIMIT = 56 * 1024 * 1024
HI = lax.Precision.HIGHEST


def _cparams(sem):
    return pltpu.CompilerParams(dimension_semantics=sem, vmem_limit_bytes=VMEM_LIMIT)


def _softplus(x):
    return jnp.maximum(x, 0.0) + jnp.log1p(jnp.exp(-jnp.abs(x)))


def _sigmoid(x):
    return 1.0 / (1.0 + jnp.exp(-x))


def _silu(x):
    return x * _sigmoid(x)


def _dot(a, b, precision=None):
    return jnp.dot(a, b, preferred_element_type=F32, precision=precision)


def _dot_nt(a, b, precision=None):
    return lax.dot_general(a, b, (((1,), (1,)), ((), ())), preferred_element_type=F32,
                           precision=precision)


def _in_proj_kernel(x_ref, nw_ref, w_ref, wba_ref, alog_ref, dtb_ref, o_ref, gb_ref, h_scr, *, n_heads):
    j = pl.program_id(1)

    @pl.when(j == 0)
    def _():
        x = x_ref[...]
        ms = jnp.mean(x * x, axis=-1, keepdims=True)
        h = (x * lax.rsqrt(ms + NORM_EPS) * nw_ref[...]).astype(BF16)
        h_scr[...] = h
        ba = _dot(h, wba_ref[...])
        lane = lax.broadcasted_iota(jnp.int32, ba.shape, 1)
        beta = _sigmoid(ba)
        g = -jnp.exp(alog_ref[...]) * _softplus(ba + dtb_ref[...])
        gb_ref[...] = jnp.where(lane < n_heads, beta, g)

    o_ref[0] = _dot(h_scr[...], w_ref[...]).astype(o_ref.dtype)


def _in_proj(x2, norm_w, w_main, w_ba, alog_l, dtb_l, tm, tn, n_heads):
    n, d = x2.shape
    npieces = w_main.shape[1] // tn
    return pl.pallas_call(
        functools.partial(_in_proj_kernel, n_heads=n_heads),
        grid=(n // tm, npieces),
        in_specs=[
            pl.BlockSpec((tm, d), lambda i, j: (i, 0)),
            pl.BlockSpec((1, d), lambda i, j: (0, 0)),
            pl.BlockSpec((d, tn), lambda i, j: (0, j)),
            pl.BlockSpec((d, LANES), lambda i, j: (0, 0)),
            pl.BlockSpec((1, LANES), lambda i, j: (0, 0)),
            pl.BlockSpec((1, LANES), lambda i, j: (0, 0)),
        ],
        out_specs=[
            pl.BlockSpec((1, tm, tn), lambda i, j: (j, i, 0)),
            pl.BlockSpec((tm, LANES), lambda i, j: (i, 0)),
        ],
        out_shape=[
            jax.ShapeDtypeStruct((npieces, n, tn), BF16),
            jax.ShapeDtypeStruct((n, LANES), F32),
        ],
        scratch_shapes=[pltpu.VMEM((tm, d), BF16)],
        compiler_params=_cparams(("arbitrary", "arbitrary")),
        name="in_proj",
    )(x2, norm_w, w_main, w_ba, alog_l, dtb_l)


def _gdn_kernel(q_ref, k_ref, v_ref, z_ref, gb_ref, gt_ref, cw_ref, nw_ref, o_ref,
                xbuf, state, *, n_heads, lc):
    t = pl.program_id(1)
    hd = HEAD_DIM
    width = n_heads * hd
    c = GDN_CHUNK

    @pl.when(t == 0)
    def _():
        xbuf[0:HALO, :] = jnp.zeros((HALO, 3 * width), F32)
        state[...] = jnp.zeros_like(state)

    xbuf[HALO:HALO + lc, 0:width] = q_ref[0, 0].astype(F32)
    xbuf[HALO:HALO + lc, width:2 * width] = k_ref[0, 0].astype(F32)
    xbuf[HALO:HALO + lc, 2 * width:3 * width] = v_ref[0, 0].astype(F32)

    row = lax.broadcasted_iota(jnp.int32, (c, c), 0)
    col = lax.broadcasted_iota(jnp.int32, (c, c), 1)
    causal = row >= col
    strict = row > col
    eye = (row == col).astype(F32)
    ltri = causal.astype(F32)
    r2 = lax.broadcasted_iota(jnp.int32, (lc, lc), 0)
    c2 = lax.broadcasted_iota(jnp.int32, (lc, lc), 1)
    utri = ((r2 <= c2) & ((r2 // c) == (c2 // c))).astype(F32)
    grow_all = _dot(gt_ref[...], utri, HI)

    def conv_silu(off):
        acc = cw_ref[0:1, off:off + hd] * xbuf[HALO - 3:HALO - 3 + lc, off:off + hd]
        for j in range(1, CONV_W):
            acc = acc + cw_ref[j:j + 1, off:off + hd] * xbuf[HALO - 3 + j:HALO - 3 + j + lc, off:off + hd]
        return _silu(acc)

    def l2n(x):
        return x * lax.rsqrt(jnp.sum(x * x, axis=-1, keepdims=True) + NORM_EPS)

    gcols = [_dot(ltri, gb_ref[ci * c:(ci + 1) * c, :], HI) for ci in range(lc // c)]

    for h in range(n_heads):
        qh = l2n(conv_silu(h * hd)) * (hd ** -0.5)
        kh = l2n(conv_silu(width + h * hd))
        vh = conv_silu(2 * width + h * hd)
        s = state[h]
        for ci in range(lc // c):
            sl = slice(ci * c, (ci + 1) * c)
            qn, kn, vv = qh[sl], kh[sl], vh[sl]
            beta_b = jnp.broadcast_to(gb_ref[sl, h:h + 1], (c, hd))
            gc_b = jnp.broadcast_to(gcols[ci][:, n_heads + h:n_heads + h + 1], (c, hd))
            gc_row = grow_all[n_heads + h:n_heads + h + 1, sl]
            eg = jnp.exp(gc_b)
            kb16 = kn.astype(BF16)
            qkk = _dot_nt(jnp.concatenate([qn, kn], axis=0).astype(BF16), kb16)
            dmat = gc_b[:, :c] - gc_row
            decay = jnp.where(causal, jnp.exp(jnp.minimum(dmat, 0.0)), 0.0)
            attn = qkk[:c] * decay
            a = jnp.where(strict, beta_b[:, :c] * qkk[c:] * decay, 0.0)
            p = eye - a
            x = _dot(a, a, HI)
            nsq = c.bit_length() - 2
            for it in range(nsq + 1):
                p = p + _dot(p, x, HI)
                if it < nsq:
                    x = _dot(x, x, HI)
            rhs = jnp.concatenate([vv * beta_b, kn * (beta_b * eg)], axis=1)
            sol = _dot(p, rhs, HI)
            u, w = sol[:, :hd], sol[:, hd:]
            wq = jnp.concatenate([w, qn * eg], axis=0).astype(BF16)
            ws = _dot(wq, s.astype(BF16))
            v_new = u - ws[:c]
            o = ws[c:] + _dot(attn.astype(BF16), v_new.astype(BF16))
            glast_row = gc_row[:, c - 1:c]
            kdt = kn.T * jnp.exp(glast_row - gc_row)
            s = s * jnp.exp(gc_b[c - 1:c, :]) + _dot(kdt.astype(BF16), v_new.astype(BF16))
            on = o * lax.rsqrt(jnp.mean(o * o, axis=-1, keepdims=True) + NORM_EPS) * nw_ref[...]
            zz = z_ref[0, 0, sl, h * hd:(h + 1) * hd].astype(F32)
            o_ref[0, sl, h * hd:(h + 1) * hd] = (on * _silu(zz)).astype(o_ref.dtype)
        state[h] = s

    xbuf[0:HALO, :] = xbuf[lc:lc + HALO, :]


def _gdn(proj4, gb, gt, conv_w, norm_w, bsz, seq, n_heads, lc):
    width = n_heads * HEAD_DIM
    nt = seq // lc
    kern = functools.partial(_gdn_kernel, n_heads=n_heads, lc=lc)

    def piece(p):
        return pl.BlockSpec((1, 1, lc, width), lambda b, t, p=p: (p, b, t, 0))

    return pl.pallas_call(
        kern,
        grid=(bsz, nt),
        in_specs=[
            piece(0), piece(1), piece(2), piece(3),
            pl.BlockSpec((lc, LANES), lambda b, t: (b * nt + t, 0)),
            pl.BlockSpec((2 * n_heads, lc), lambda b, t: (0, b * nt + t)),
            pl.BlockSpec((CONV_W, 3 * width), lambda b, t: (0, 0)),
            pl.BlockSpec((1, HEAD_DIM), lambda b, t: (0, 0)),
        ],
        out_specs=pl.BlockSpec((1, lc, width), lambda b, t: (b, t, 0)),
        out_shape=jax.ShapeDtypeStruct((bsz, seq, width), BF16),
        scratch_shapes=[
            pltpu.VMEM((lc + HALO, 3 * width), F32),
            pltpu.VMEM((n_heads, HEAD_DIM, HEAD_DIM), F32),
        ],
        compiler_params=_cparams(("arbitrary", "arbitrary")),
        name="gdn",
    )(proj4, proj4, proj4, proj4, gb, gt, conv_w, norm_w)


def _rglru_kernel(x_ref, g_ref, cw_ref, cb_ref, wri_ref, br_ref, bi_ref, lam_ref, nw_ref, o_ref,
                  xbuf, a_scr, b_scr, hlast, *, n_blocks, lr):
    t = pl.program_id(1)
    bd = LANES
    width = n_blocks * bd

    @pl.when(t == 0)
    def _():
        xbuf[0:HALO, :] = jnp.zeros((HALO, width), F32)
        hlast[...] = jnp.zeros_like(hlast)

    xbuf[HALO:HALO + lr, :] = x_ref[0, 0].astype(F32)
    xc = cw_ref[0:1, :] * xbuf[HALO - 3:HALO - 3 + lr, :]
    for j in range(1, CONV_W):
        xc = xc + cw_ref[j:j + 1, :] * xbuf[HALO - 3 + j:HALO - 3 + j + lr, :]
    xc = xc + cb_ref[...]
    xbuf[0:HALO, :] = xbuf[lr:lr + HALO, :]

    sp_lam = _softplus(-lam_ref[...])
    sub = lax.broadcasted_iota(jnp.int32, (lr, bd), 0) % SUBLANES
    for n in range(n_blocks):
        sl = slice(n * bd, (n + 1) * bd)
        xb = xc[:, sl]
        gates = _dot(xb.astype(BF16), wri_ref[n])
        r = _sigmoid(gates[:, :bd] + br_ref[:, sl])
        i = _sigmoid(gates[:, bd:] + bi_ref[:, sl])
        log_a = -LRU_C * r * sp_lam[:, sl]
        a = jnp.exp(log_a)
        th = jnp.tanh(log_a)
        b = jnp.sqrt(-2.0 * th / (1.0 - th)) * (i * xb)
        for s in (1, 2, 4):
            a_s = pltpu.roll(a, s, axis=0)
            b_s = pltpu.roll(b, s, axis=0)
            m = sub >= s
            b = jnp.where(m, a * b_s + b, b)
            a = jnp.where(m, a * a_s, a)
        a_scr[:, sl] = a
        b_scr[:, sl] = b

    def tile_step(i, hprev):
        r0 = pl.multiple_of(i * SUBLANES, SUBLANES)
        hh = a_scr[pl.ds(r0, SUBLANES), :] * hprev + b_scr[pl.ds(r0, SUBLANES), :]
        b_scr[pl.ds(r0, SUBLANES), :] = hh
        return jnp.broadcast_to(hh[SUBLANES - 1:SUBLANES, :], (SUBLANES, width))

    hfin = lax.fori_loop(0, lr // SUBLANES, tile_step, hlast[...])
    hlast[...] = hfin

    for n in range(n_blocks):
        sl = slice(n * bd, (n + 1) * bd)
        gt = g_ref[0, 0, :, sl].astype(F32)
        y = b_scr[:, sl] * jax.nn.gelu(gt, approximate=True)
        yn = y * lax.rsqrt(jnp.mean(y * y, axis=-1, keepdims=True) + NORM_EPS) * nw_ref[:, sl]
        o_ref[0, :, sl] = yn.astype(o_ref.dtype)


def _rglru(proj4, conv_w, conv_b, wri, br, bi, lam, norm_w, bsz, seq, n_blocks, lr, px, pg):
    width = n_blocks * LANES
    nt = seq // lr
    kern = functools.partial(_rglru_kernel, n_blocks=n_blocks, lr=lr)

    def vec():
        return pl.BlockSpec((1, width), lambda b, t: (0, 0))

    return pl.pallas_call(
        kern,
        grid=(bsz, nt),
        in_specs=[
            pl.BlockSpec((1, 1, lr, width), lambda b, t: (px, b, t, 0)),
            pl.BlockSpec((1, 1, lr, width), lambda b, t: (pg, b, t, 0)),
            pl.BlockSpec((CONV_W, width), lambda b, t: (0, 0)),
            vec(),
            pl.BlockSpec((n_blocks, LANES, 2 * LANES), lambda b, t: (0, 0, 0)),
            vec(), vec(), vec(), vec(),
        ],
        out_specs=pl.BlockSpec((1, lr, width), lambda b, t: (b, t, 0)),
        out_shape=jax.ShapeDtypeStruct((bsz, seq, width), BF16),
        scratch_shapes=[
            pltpu.VMEM((lr + HALO, width), F32),
            pltpu.VMEM((lr, width), F32),
            pltpu.VMEM((lr, width), F32),
            pltpu.VMEM((SUBLANES, width), F32),
        ],
        compiler_params=_cparams(("arbitrary", "arbitrary")),
        name="rglru",
    )(proj4, proj4, conv_w, conv_b, wri, br, bi, lam, norm_w)


def _out_route_kernel(yg_ref, yr_ref, x_ref, wog_ref, wor_ref, nw_ref, wr_ref, rb_ref,
                      x1_ref, h2_ref, route_ref, cnt_ref, cnt_scr, *, n_groups, per_group):
    i = pl.program_id(0)
    tm = x_ref.shape[0]

    @pl.when(i == 0)
    def _():
        cnt_scr[...] = jnp.zeros_like(cnt_scr)

    x1 = x_ref[...] + _dot(yg_ref[...], wog_ref[...]) + _dot(yr_ref[...], wor_ref[...])
    x1_ref[...] = x1
    h2 = x1 * lax.rsqrt(jnp.mean(x1 * x1, axis=-1, keepdims=True) + NORM_EPS) * nw_ref[...]
    h2_ref[...] = h2
    logits = _dot(h2, wr_ref[...], HI) + rb_ref[...]
    lane = lax.broadcasted_iota(jnp.int32, logits.shape, 1)
    neg = jnp.float32(-jnp.inf)
    big = jnp.int32(1 << 20)

    def first_argmax(vals, mask):
        mx = jnp.max(jnp.where(mask, vals, neg), axis=-1, keepdims=True)
        idx = jnp.min(jnp.where(mask & (vals == mx), lane, big), axis=-1, keepdims=True)
        return mx, idx

    gmask = lane < n_groups
    gmax, gidx = first_argmax(logits, gmask)
    zg = jnp.sum(jnp.where(gmask, jnp.exp(logits - gmax), 0.0), axis=-1, keepdims=True)
    grp_p = 1.0 / zg
    lo = n_groups + gidx * per_group
    emask = (lane >= lo) & (lane < lo + per_group)
    m1, i1 = first_argmax(logits, emask)
    m2, i2 = first_argmax(logits, emask & (lane != i1))
    ze = jnp.sum(jnp.where(emask, jnp.exp(logits - m1), 0.0), axis=-1, keepdims=True)
    p1 = 1.0 / ze
    p2 = jnp.exp(m2 - m1) / ze
    den = p1 + p2
    g1 = grp_p * (p1 / den)
    g2 = grp_p * (p2 / den)
    e1 = i1 - n_groups
    e2 = i2 - n_groups
    oh1 = lane == e1
    oh2 = lane == e2
    onehot = jnp.where(oh1 | oh2, 1.0, 0.0)
    r = lax.broadcasted_iota(jnp.int32, (tm, tm), 0)
    c = lax.broadcasted_iota(jnp.int32, (tm, tm), 1)
    prefix = _dot((r > c).astype(BF16), onehot.astype(BF16)) + cnt_scr[...]
    rank1 = jnp.sum(jnp.where(oh1, prefix, 0.0), axis=-1, keepdims=True)
    rank2 = jnp.sum(jnp.where(oh2, prefix, 0.0), axis=-1, keepdims=True)
    cnt_scr[...] = cnt_scr[...] + jnp.sum(onehot, axis=0, keepdims=True)
    cnt_ref[...] = cnt_scr[...]
    out = jnp.where(lane == 0, e1.astype(F32), 0.0)
    out = jnp.where(lane == 1, e2.astype(F32), out)
    out = jnp.where(lane == 2, rank1, out)
    out = jnp.where(lane == 3, rank2, out)
    out = jnp.where(lane == 4, g1, out)
    out = jnp.where(lane == 5, g2, out)
    route_ref[...] = out


def _out_route(y_gdn, y_rg, x2, wo_g, wo_r, norm_w, w_router, b_router, tm, n_groups, per_group):
    n, d = x2.shape
    wg = y_gdn.shape[1]
    wr = y_rg.shape[1]
    kern = functools.partial(_out_route_kernel, n_groups=n_groups, per_group=per_group)
    return pl.pallas_call(
        kern,
        grid=(n // tm,),
        in_specs=[
            pl.BlockSpec((tm, wg), lambda i: (i, 0)),
            pl.BlockSpec((tm, wr), lambda i: (i, 0)),
            pl.BlockSpec((tm, d), lambda i: (i, 0)),
            pl.BlockSpec((wg, d), lambda i: (0, 0)),
            pl.BlockSpec((wr, d), lambda i: (0, 0)),
            pl.BlockSpec((1, d), lambda i: (0, 0)),
            pl.BlockSpec((d, LANES), lambda i: (0, 0)),
            pl.BlockSpec((1, LANES), lambda i: (0, 0)),
        ],
        out_specs=[
            pl.BlockSpec((tm, d), lambda i: (i, 0)),
            pl.BlockSpec((tm, d), lambda i: (i, 0)),
            pl.BlockSpec((tm, LANES), lambda i: (i, 0)),
            pl.BlockSpec((1, LANES), lambda i: (0, 0)),
        ],
        out_shape=[
            jax.ShapeDtypeStruct((n, d), F32),
            jax.ShapeDtypeStruct((n, d), F32),
            jax.ShapeDtypeStruct((n, LANES), F32),
            jax.ShapeDtypeStruct((1, LANES), F32),
        ],
        scratch_shapes=[pltpu.VMEM((1, LANES), F32)],
        compiler_params=_cparams(("arbitrary",)),
        name="out_route",
    )(y_gdn, y_rg, x2, wo_g, wo_r, norm_w, w_router, b_router)


def _dispatch_kernel(pos1_ref, pos2_ref, pad_ref, h_ref, xs_ref, zeros, sem, zsem, *, n_pad):
    i = pl.program_id(0)
    tm = h_ref.shape[0]
    tz = zeros.shape[0]

    @pl.when(i == 0)
    def _():
        zeros[...] = jnp.zeros_like(zeros)

        def zero_copy(e):
            row = pl.multiple_of(jnp.maximum(pad_ref[e], 0), SUBLANES)
            return pltpu.make_async_copy(zeros, xs_ref.at[pl.ds(row, tz), :], zsem)

        for e in range(n_pad):
            @pl.when(pad_ref[e] >= 0)
            def _():
                zero_copy(e).start()
        for e in range(n_pad):
            @pl.when(pad_ref[e] >= 0)
            def _():
                zero_copy(e).wait()

    def row_copy(r, pos_ref):
        return pltpu.make_async_copy(h_ref.at[pl.ds(r, 1), :],
                                     xs_ref.at[pl.ds(pos_ref[i * tm + r], 1), :], sem)

    def issue(r, carry):
        row_copy(r, pos1_ref).start()
        row_copy(r, pos2_ref).start()
        return carry

    lax.fori_loop(0, tm, issue, 0)

    def drain(r, carry):
        row_copy(r, pos1_ref).wait()
        row_copy(r, pos2_ref).wait()
        return carry

    lax.fori_loop(0, tm, drain, 0)


def _dispatch(pos1, pos2, pad_row, h2, n_rows, tm, tm_e):
    n, d = h2.shape
    return pl.pallas_call(
        functools.partial(_dispatch_kernel, n_pad=pad_row.shape[0]),
        grid_spec=pltpu.PrefetchScalarGridSpec(
            num_scalar_prefetch=3,
            grid=(n // tm,),
            in_specs=[pl.BlockSpec((tm, d), lambda i, p1, p2, pr: (i, 0))],
            out_specs=pl.BlockSpec(memory_space=pl.ANY),
            scratch_shapes=[pltpu.VMEM((tm_e, d), h2.dtype),
                            pltpu.SemaphoreType.DMA(()), pltpu.SemaphoreType.DMA(())],
        ),
        out_shape=jax.ShapeDtypeStruct((n_rows, d), h2.dtype),
        compiler_params=pltpu.CompilerParams(dimension_semantics=("arbitrary",),
                                             vmem_limit_bytes=VMEM_LIMIT, has_side_effects=True),
        name="dispatch",
    )(pos1, pos2, pad_row, h2)


def _experts_kernel(te_ref, nt_ref, xs_ref, w1_ref, w3_ref, w2_ref, ys_ref, w1b, w3b, w2b):
    j = pl.program_id(0)
    prev = te_ref[jnp.maximum(j - 1, 0)]
    active = j < nt_ref[0]

    @pl.when(active & ((j == 0) | (te_ref[j] != prev)))
    def _():
        w1b[...] = w1_ref[0].astype(BF16)
        w3b[...] = w3_ref[0].astype(BF16)
        w2b[...] = w2_ref[0].astype(BF16)

    @pl.when(active)
    def _():
        x = xs_ref[...].astype(BF16)
        h1 = _dot(x, w1b[...])
        h3 = _dot(x, w3b[...])
        he = (_silu(h1) * h3).astype(BF16)
        ys_ref[...] = _dot(he, w2b[...])

    @pl.when(jnp.logical_not(active))
    def _():
        ys_ref[...] = jnp.zeros_like(ys_ref)


def _experts(tile_expert, n_tiles, xs, w1, w3, w2, tm):
    n_rows, d = xs.shape
    ff = w1.shape[2]

    def wmap(j, te, nt):
        return (te[j], 0, 0)

    def rowmap(j, te, nt):
        return (jnp.minimum(j, nt[0] - 1), 0)

    return pl.pallas_call(
        _experts_kernel,
        grid_spec=pltpu.PrefetchScalarGridSpec(
            num_scalar_prefetch=2,
            grid=(n_rows // tm,),
            in_specs=[
                pl.BlockSpec((tm, d), rowmap),
                pl.BlockSpec((1, d, ff), wmap),
                pl.BlockSpec((1, d, ff), wmap),
                pl.BlockSpec((1, ff, d), wmap),
            ],
            out_specs=pl.BlockSpec((tm, d), lambda j, te, nt: (j, 0)),
            scratch_shapes=[
                pltpu.VMEM((d, ff), BF16),
                pltpu.VMEM((d, ff), BF16),
                pltpu.VMEM((ff, d), BF16),
            ],
        ),
        out_shape=jax.ShapeDtypeStruct((n_rows, d), F32),
        compiler_params=_cparams(("arbitrary",)),
        name="experts",
    )(tile_expert, n_tiles, xs, w1, w3, w2)


def _combine_kernel(pos1_ref, pos2_ref, x1_ref, route_ref, ys_ref, nw_ref, o_ref, y1, y2, sem):
    i = pl.program_id(0)
    tm = x1_ref.shape[0]

    def row_copy(r, pos_ref, dst):
        return pltpu.make_async_copy(ys_ref.at[pl.ds(pos_ref[i * tm + r], 1), :],
                                     dst.at[pl.ds(r, 1), :], sem)

    def issue(r, carry):
        row_copy(r, pos1_ref, y1).start()
        row_copy(r, pos2_ref, y2).start()
        return carry

    lax.fori_loop(0, tm, issue, 0)

    def drain(r, carry):
        row_copy(r, pos1_ref, y1).wait()
        row_copy(r, pos2_ref, y2).wait()
        return carry

    lax.fori_loop(0, tm, drain, 0)

    route = route_ref[...]
    g1 = route[:, 4:5]
    g2 = route[:, 5:6]
    x = x1_ref[...] + (g1 * y1[...] + g2 * y2[...])
    o_ref[...] = x * lax.rsqrt(jnp.mean(x * x, axis=-1, keepdims=True) + NORM_EPS) * nw_ref[...]


def _combine(pos1, pos2, x1, route, ys, norm_w, tm):
    n, d = x1.shape
    return pl.pallas_call(
        _combine_kernel,
        grid_spec=pltpu.PrefetchScalarGridSpec(
            num_scalar_prefetch=2,
            grid=(n // tm,),
            in_specs=[
                pl.BlockSpec((tm, d), lambda i, p1, p2: (i, 0)),
                pl.BlockSpec((tm, LANES), lambda i, p1, p2: (i, 0)),
                pl.BlockSpec(memory_space=pl.ANY),
                pl.BlockSpec((1, d), lambda i, p1, p2: (0, 0)),
            ],
            out_specs=pl.BlockSpec((tm, d), lambda i, p1, p2: (i, 0)),
            scratch_shapes=[
                pltpu.VMEM((tm, d), F32),
                pltpu.VMEM((tm, d), F32),
                pltpu.SemaphoreType.DMA(()),
            ],
        ),
        out_shape=jax.ShapeDtypeStruct((n, d), F32),
        compiler_params=_cparams(("arbitrary",)),
        name="combine",
    )(pos1, pos2, x1, route, ys, norm_w)


def _pick(n, pref):
    t = min(n, pref)
    assert n % t == 0, (n, pref)
    return t


def _lane_vec(v, offset=0):
    return jnp.zeros((1, LANES), F32).at[0, offset:offset + v.shape[0]].set(v.astype(F32))


def kernel(x, norm_mix_w, w_in, gdn_conv_w, gdn_a_log, gdn_dt_bias, gdn_norm_w, rg_conv_w, rg_conv_b, rg_w_rgate, rg_b_rgate, rg_w_igate, rg_b_igate, rg_lambda, rg_norm_w, w_out, norm_ffn_w, router_w_group, router_b_group, router_w_expert, router_b_expert, expert_w1, expert_w3, expert_w2, norm_final_w):
    bsz, seq, d = x.shape
    n = bsz * seq
    depth = w_in.shape[0]
    assert depth == 1
    n_heads = gdn_a_log.shape[1]
    gw = n_heads * HEAD_DIM
    rw = rg_lambda.shape[1]
    n_blocks = rg_w_rgate.shape[1]
    n_groups = router_w_group.shape[2]
    n_experts = expert_w1.shape[1]
    per_group = n_experts // n_groups
    assert gw == rw and rw == n_blocks * LANES
    assert n_groups + n_experts <= LANES and 2 * n_heads <= LANES

    x2 = x.reshape(n, d)
    wi = w_in[0]
    nb = 4 * gw
    w_main = jnp.concatenate([wi[:, :nb], wi[:, nb + 2 * n_heads:]], axis=1).astype(BF16)
    w_ba = jnp.pad(wi[:, nb:nb + 2 * n_heads], ((0, 0), (0, LANES - 2 * n_heads))).astype(BF16)
    alog_l = _lane_vec(gdn_a_log[0], n_heads)
    dtb_l = _lane_vec(gdn_dt_bias[0], n_heads)

    tm_in = _pick(n, 512)
    proj, gb = _in_proj(x2, norm_mix_w[0][None, :], w_main, w_ba, alog_l, dtb_l, tm_in, gw, n_heads)
    proj4 = proj.reshape(proj.shape[0], bsz, seq, gw)
    gt = gb[:, :2 * n_heads].T

    lc = _pick(seq, 2 * GDN_CHUNK)
    y_gdn = _gdn(proj4, gb, gt, gdn_conv_w[0], gdn_norm_w[0][None, :], bsz, seq, n_heads, lc)

    wri = jnp.concatenate([rg_w_rgate[0], rg_w_igate[0]], axis=-1).astype(BF16)
    lr = _pick(seq, 256)
    y_rg = _rglru(proj4, rg_conv_w[0], rg_conv_b[0][None, :], wri, rg_b_rgate[0][None, :],
                  rg_b_igate[0][None, :], rg_lambda[0][None, :], rg_norm_w[0][None, :],
                  bsz, seq, n_blocks, lr, 4, 5)

    wo = w_out[0].astype(BF16)
    w_router = jnp.pad(jnp.concatenate([router_w_group[0], router_w_expert[0]], axis=1),
                       ((0, 0), (0, LANES - n_groups - n_experts)))
    b_router = _lane_vec(jnp.concatenate([router_b_group[0], router_b_expert[0]]))
    tm_o = _pick(n, 256)
    x1, h2, route, counts = _out_route(y_gdn.reshape(n, gw), y_rg.reshape(n, rw), x2, wo[:gw], wo[gw:],
                                       norm_ffn_w[0][None, :], w_router, b_router, tm_o,
                                       n_groups, per_group)

    tm_e = 256
    cnt = counts[0, :n_experts].astype(jnp.int32)
    tiles_per = (cnt + tm_e - 1) // tm_e
    tile_end = jnp.cumsum(tiles_per)
    tile_start = tile_end - tiles_per
    max_tiles = (2 * n) // tm_e + n_experts
    n_rows = max_tiles * tm_e
    n_tiles = tile_end[-1:].astype(jnp.int32)
    jt = jnp.minimum(jnp.arange(max_tiles, dtype=jnp.int32), n_tiles[0] - 1)
    tile_expert = jnp.searchsorted(tile_end, jt, side="right").astype(jnp.int32)
    row_start = tile_start * tm_e
    tail = n_tiles[0] + jnp.arange(n_experts, dtype=jnp.int32)
    pad_row = jnp.concatenate([jnp.where(tiles_per > 0, (tile_end - 1) * tm_e, -1),
                               jnp.where(tail < max_tiles, tail * tm_e, -1)]).astype(jnp.int32)
    e1 = route[:, 0].astype(jnp.int32)
    e2 = route[:, 1].astype(jnp.int32)
    pos1 = row_start[e1] + route[:, 2].astype(jnp.int32)
    pos2 = row_start[e2] + route[:, 3].astype(jnp.int32)

    tm_d = _pick(n, 256)
    xs = _dispatch(pos1, pos2, pad_row, h2, n_rows, tm_d, tm_e)
    ys = _experts(tile_expert, n_tiles, xs, expert_w1[0], expert_w3[0], expert_w2[0], tm_e)
    out = _combine(pos1, pos2, x1, route, ys, norm_final_w[None, :], tm_d)
    return out.reshape(bsz, seq, d)
```

```python
import functools

import jax
import jax.numpy as jnp
from jax import lax
from jax.experimental import pallas as pl
from jax.experimental.pallas import tpu as pltpu

F32 = jnp.float32
BF16 = jnp.bfloat16
NORM_EPS = 1e-6
LRU_C = 8.0
LANES = 128
SUBLANES = 8
GDN_CHUNK = 64
HEAD_DIM = 128
CONV_W = 4
HALO = SUBLANES
VMEM_LIMIT = 56 * 1024 * 1024
HI = lax.Precision.HIGHEST


def _cparams(sem):
    return pltpu.CompilerParams(dimension_semantics=sem, vmem_limit_bytes=VMEM_LIMIT)


def _softplus(x):
    return jnp.maximum(x, 0.0) + jnp.log1p(jnp.exp(-jnp.abs(x)))


def _sigmoid(x):
    return 1.0 / (1.0 + jnp.exp(-x))


def _silu(x):
    return x * _sigmoid(x)


def _dot(a, b, precision=None):
    return jnp.dot(a, b, preferred_element_type=F32, precision=precision)


def _dot_nt(a, b, precision=None):
    return lax.dot_general(a, b, (((1,), (1,)), ((), ())), preferred_element_type=F32,
                           precision=precision)


def _in_proj_kernel(x_ref, nw_ref, w_ref, wba_ref, alog_ref, dtb_ref, o_ref, gb_ref, h_scr, *, n_heads):
    j = pl.program_id(1)

    @pl.when(j == 0)
    def _():
        x = x_ref[...]
        ms = jnp.mean(x * x, axis=-1, keepdims=True)
        h = (x * lax.rsqrt(ms + NORM_EPS) * nw_ref[...]).astype(BF16)
        h_scr[...] = h
        ba = _dot(h, wba_ref[...])
        lane = lax.broadcasted_iota(jnp.int32, ba.shape, 1)
        beta = _sigmoid(ba)
        g = -jnp.exp(alog_ref[...]) * _softplus(ba + dtb_ref[...])
        gb_ref[...] = jnp.where(lane < n_heads, beta, g)

    o_ref[0] = _dot(h_scr[...], w_ref[...]).astype(o_ref.dtype)


def _in_proj(x2, norm_w, w_main, w_ba, alog_l, dtb_l, tm, tn, n_heads):
    n, d = x2.shape
    npieces = w_main.shape[1] // tn
    return pl.pallas_call(
        functools.partial(_in_proj_kernel, n_heads=n_heads),
        grid=(n // tm, npieces),
        in_specs=[
            pl.BlockSpec((tm, d), lambda i, j: (i, 0)),
            pl.BlockSpec((1, d), lambda i, j: (0, 0)),
            pl.BlockSpec((d, tn), lambda i, j: (0, j)),
            pl.BlockSpec((d, LANES), lambda i, j: (0, 0)),
            pl.BlockSpec((1, LANES), lambda i, j: (0, 0)),
            pl.BlockSpec((1, LANES), lambda i, j: (0, 0)),
        ],
        out_specs=[
            pl.BlockSpec((1, tm, tn), lambda i, j: (j, i, 0)),
            pl.BlockSpec((tm, LANES), lambda i, j: (i, 0)),
        ],
        out_shape=[
            jax.ShapeDtypeStruct((npieces, n, tn), BF16),
            jax.ShapeDtypeStruct((n, LANES), F32),
        ],
        scratch_shapes=[pltpu.VMEM((tm, d), BF16)],
        compiler_params=_cparams(("arbitrary", "arbitrary")),
        name="in_proj",
    )(x2, norm_w, w_main, w_ba, alog_l, dtb_l)


def _gdn_kernel(q_ref, k_ref, v_ref, z_ref, gb_ref, gt_ref, cw_ref, nw_ref, o_ref,
                xbuf, qkv, state, *, n_heads, lc):
    t = pl.program_id(1)
    hd = HEAD_DIM
    width = n_heads * hd
    c = GDN_CHUNK

    @pl.when(t == 0)
    def _():
        xbuf[0:HALO, :] = jnp.zeros((HALO, 3 * width), F32)
        state[...] = jnp.zeros_like(state)

    xbuf[HALO:HALO + lc, 0:width] = q_ref[0, 0].astype(F32)
    xbuf[HALO:HALO + lc, width:2 * width] = k_ref[0, 0].astype(F32)
    xbuf[HALO:HALO + lc, 2 * width:3 * width] = v_ref[0, 0].astype(F32)

    row = lax.broadcasted_iota(jnp.int32, (c, c), 0)
    col = lax.broadcasted_iota(jnp.int32, (c, c), 1)
    causal = row >= col
    strict = row > col
    eye = (row == col).astype(F32)
    ltri = causal.astype(F32)
    r2 = lax.broadcasted_iota(jnp.int32, (lc, lc), 0)
    c2 = lax.broadcasted_iota(jnp.int32, (lc, lc), 1)
    utri = ((r2 <= c2) & ((r2 // c) == (c2 // c))).astype(F32)
    grow_all = _dot(gt_ref[...], utri, HI)

    def conv_silu(off):
        acc = cw_ref[0:1, off:off + hd] * xbuf[HALO - 3:HALO - 3 + lc, off:off + hd]
        for j in range(1, CONV_W):
            acc = acc + cw_ref[j:j + 1, off:off + hd] * xbuf[HALO - 3 + j:HALO - 3 + j + lc, off:off + hd]
        return _silu(acc)

    def l2n(x):
        return x * lax.rsqrt(jnp.sum(x * x, axis=-1, keepdims=True) + NORM_EPS)

    nci = lc // c
    gcols = [_dot(ltri, gb_ref[ci * c:(ci + 1) * c, :], HI) for ci in range(nci)]

    for h in range(n_heads):
        qkv[:, h * hd:(h + 1) * hd] = l2n(conv_silu(h * hd)) * (hd ** -0.5)
        qkv[:, width + h * hd:width + (h + 1) * hd] = l2n(conv_silu(width + h * hd))
        qkv[:, 2 * width + h * hd:2 * width + (h + 1) * hd] = conv_silu(2 * width + h * hd)
    xbuf[0:HALO, :] = xbuf[lc:lc + HALO, :]

    units = [(ci, h) for ci in range(nci) for h in range(n_heads)]
    blk = SUBLANES * 2
    diag_blk = (row // blk) == (col // blk)
    bf = lambda v: v.astype(BF16)

    st = {}
    for (ci, h) in units:
        sl = slice(ci * c, (ci + 1) * c)
        qn = qkv[sl, h * hd:(h + 1) * hd]
        kn = qkv[sl, width + h * hd:width + (h + 1) * hd]
        vv = qkv[sl, 2 * width + h * hd:2 * width + (h + 1) * hd]
        beta_b = jnp.broadcast_to(gb_ref[sl, h:h + 1], (c, hd))
        gc_b = jnp.broadcast_to(gcols[ci][:, n_heads + h:n_heads + h + 1], (c, hd))
        gc_row = grow_all[n_heads + h:n_heads + h + 1, sl]
        eg = jnp.exp(gc_b)
        qkk = _dot_nt(bf(jnp.concatenate([qn, kn], axis=0)), bf(kn))
        dmat = gc_b[:, :c] - gc_row
        decay = jnp.where(causal, jnp.exp(jnp.minimum(dmat, 0.0)), 0.0)
        a = jnp.where(strict, beta_b[:, :c] * qkk[c:] * decay, 0.0)
        dm = jnp.where(diag_blk, a, 0.0)
        st[(ci, h)] = dict(
            attn=bf(qkk[:c] * decay), dm=dm, lm=bf(a - dm),
            rhs=bf(jnp.concatenate([vv * beta_b, kn * (beta_b * eg)], axis=1)),
            qg=qn * eg,
            kdt=bf(kn.T * jnp.exp(gc_row[:, c - 1:c] - gc_row)),
            sdec=jnp.exp(gc_b[c - 1:c, :]))

    for u in units:
        d = st[u]
        d["x"] = _dot(bf(d["dm"]), bf(d["dm"]))
        d["p"] = eye - d["dm"]
    nsq = blk.bit_length() - 2
    for it in range(nsq):
        for u in units:
            d = st[u]
            if it < nsq - 1:
                px = _dot(bf(jnp.concatenate([d["p"], d["x"]], axis=0)), bf(d["x"]))
                d["p"] = d["p"] + px[:c]
                d["x"] = px[c:]
            else:
                d["p"] = d["p"] + _dot(bf(d["p"]), bf(d["x"]))
    for u in units:
        d = st[u]
        pb = bf(d["p"])
        d["m"] = _dot(pb, d["lm"])
        d["y"] = _dot(pb, d["rhs"])
    npow = c // blk - 1
    for u in units:
        d = st[u]
        d["tp"] = eye - d["m"]
        d["mk"] = d["m"]
    for k in range(2, npow + 1):
        for u in units:
            d = st[u]
            d["mk"] = _dot(bf(d["m"]), bf(d["mk"]))
            d["tp"] = d["tp"] + d["mk"] if k % 2 == 0 else d["tp"] - d["mk"]
    for u in units:
        d = st[u]
        d["sol"] = _dot(bf(d["tp"]), bf(d["y"]))

    states = [state[h] for h in range(n_heads)]
    for ci in range(nci):
        sl = slice(ci * c, (ci + 1) * c)
        ws = {}
        for h in range(n_heads):
            d = st[(ci, h)]
            wq = bf(jnp.concatenate([d["sol"][:, hd:], d["qg"]], axis=0))
            ws[h] = _dot(wq, bf(states[h]))
        upd = {}
        vnew = {}
        for h in range(n_heads):
            d = st[(ci, h)]
            vnew[h] = bf(d["sol"][:, :hd] - ws[h][:c])
            upd[h] = _dot(jnp.concatenate([d["attn"], d["kdt"]], axis=0), vnew[h])
        for h in range(n_heads):
            d = st[(ci, h)]
            o = ws[h][c:] + upd[h][:c]
            states[h] = states[h] * d["sdec"] + upd[h][c:]
            on = o * lax.rsqrt(jnp.mean(o * o, axis=-1, keepdims=True) + NORM_EPS) * nw_ref[...]
            zz = z_ref[0, 0, sl, h * hd:(h + 1) * hd].astype(F32)
            o_ref[0, sl, h * hd:(h + 1) * hd] = (on * _silu(zz)).astype(o_ref.dtype)
    for h in range(n_heads):
        state[h] = states[h]


def _gdn(proj4, gb, gt, conv_w, norm_w, bsz, seq, n_heads, lc):
    width = n_heads * HEAD_DIM
    nt = seq // lc
    kern = functools.partial(_gdn_kernel, n_heads=n_heads, lc=lc)

    def piece(p):
        return pl.BlockSpec((1, 1, lc, width), lambda b, t, p=p: (p, b, t, 0))

    return pl.pallas_call(
        kern,
        grid=(bsz, nt),
        in_specs=[
            piece(0), piece(1), piece(2), piece(3),
            pl.BlockSpec((lc, LANES), lambda b, t: (b * nt + t, 0)),
            pl.BlockSpec((2 * n_heads, lc), lambda b, t: (0, b * nt + t)),
            pl.BlockSpec((CONV_W, 3 * width), lambda b, t: (0, 0)),
            pl.BlockSpec((1, HEAD_DIM), lambda b, t: (0, 0)),
        ],
        out_specs=pl.BlockSpec((1, lc, width), lambda b, t: (b, t, 0)),
        out_shape=jax.ShapeDtypeStruct((bsz, seq, width), BF16),
        scratch_shapes=[
            pltpu.VMEM((lc + HALO, 3 * width), F32),
            pltpu.VMEM((lc, 3 * width), F32),
            pltpu.VMEM((n_heads, HEAD_DIM, HEAD_DIM), F32),
        ],
        compiler_params=_cparams(("arbitrary", "arbitrary")),
        name="gdn",
    )(proj4, proj4, proj4, proj4, gb, gt, conv_w, norm_w)


def _rglru_kernel(x_ref, g_ref, cw_ref, cb_ref, wri_ref, br_ref, bi_ref, lam_ref, nw_ref, o_ref,
                  xbuf, a_scr, b_scr, hlast, *, n_blocks, lr):
    t = pl.program_id(1)
    bd = LANES
    width = n_blocks * bd

    @pl.when(t == 0)
    def _():
        xbuf[0:HALO, :] = jnp.zeros((HALO, width), F32)
        hlast[...] = jnp.zeros_like(hlast)

    xbuf[HALO:HALO + lr, :] = x_ref[0, 0].astype(F32)
    xc = cw_ref[0:1, :] * xbuf[HALO - 3:HALO - 3 + lr, :]
    for j in range(1, CONV_W):
        xc = xc + cw_ref[j:j + 1, :] * xbuf[HALO - 3 + j:HALO - 3 + j + lr, :]
    xc = xc + cb_ref[...]
    xbuf[0:HALO, :] = xbuf[lr:lr + HALO, :]

    sp_lam = _softplus(-lam_ref[...])
    sub = lax.broadcasted_iota(jnp.int32, (lr, bd), 0) % SUBLANES
    for n in range(n_blocks):
        sl = slice(n * bd, (n + 1) * bd)
        xb = xc[:, sl]
        gates = _dot(xb.astype(BF16), wri_ref[n])
        r = _sigmoid(gates[:, :bd] + br_ref[:, sl])
        i = _sigmoid(gates[:, bd:] + bi_ref[:, sl])
        log_a = -LRU_C * r * sp_lam[:, sl]
        a = jnp.exp(log_a)
        th = jnp.tanh(log_a)
        b = jnp.sqrt(-2.0 * th / (1.0 - th)) * (i * xb)
        for s in (1, 2, 4):
            a_s = pltpu.roll(a, s, axis=0)
            b_s = pltpu.roll(b, s, axis=0)
            m = sub >= s
            b = jnp.where(m, a * b_s + b, b)
            a = jnp.where(m, a * a_s, a)
        a_scr[:, sl] = a
        b_scr[:, sl] = b

    def tile_step(i, hprev):
        r0 = pl.multiple_of(i * SUBLANES, SUBLANES)
        hh = a_scr[pl.ds(r0, SUBLANES), :] * hprev + b_scr[pl.ds(r0, SUBLANES), :]
        b_scr[pl.ds(r0, SUBLANES), :] = hh
        return jnp.broadcast_to(hh[SUBLANES - 1:SUBLANES, :], (SUBLANES, width))

    hfin = lax.fori_loop(0, lr // SUBLANES, tile_step, hlast[...])
    hlast[...] = hfin

    for n in range(n_blocks):
        sl = slice(n * bd, (n + 1) * bd)
        gt = g_ref[0, 0, :, sl].astype(F32)
        y = b_scr[:, sl] * jax.nn.gelu(gt, approximate=True)
        yn = y * lax.rsqrt(jnp.mean(y * y, axis=-1, keepdims=True) + NORM_EPS) * nw_ref[:, sl]
        o_ref[0, :, sl] = yn.astype(o_ref.dtype)


def _rglru(proj4, conv_w, conv_b, wri, br, bi, lam, norm_w, bsz, seq, n_blocks, lr, px, pg):
    width = n_blocks * LANES
    nt = seq // lr
    kern = functools.partial(_rglru_kernel, n_blocks=n_blocks, lr=lr)

    def vec():
        return pl.BlockSpec((1, width), lambda b, t: (0, 0))

    return pl.pallas_call(
        kern,
        grid=(bsz, nt),
        in_specs=[
            pl.BlockSpec((1, 1, lr, width), lambda b, t: (px, b, t, 0)),
            pl.BlockSpec((1, 1, lr, width), lambda b, t: (pg, b, t, 0)),
            pl.BlockSpec((CONV_W, width), lambda b, t: (0, 0)),
            vec(),
            pl.BlockSpec((n_blocks, LANES, 2 * LANES), lambda b, t: (0, 0, 0)),
            vec(), vec(), vec(), vec(),
        ],
        out_specs=pl.BlockSpec((1, lr, width), lambda b, t: (b, t, 0)),
        out_shape=jax.ShapeDtypeStruct((bsz, seq, width), BF16),
        scratch_shapes=[
            pltpu.VMEM((lr + HALO, width), F32),
            pltpu.VMEM((lr, width), F32),
            pltpu.VMEM((lr, width), F32),
            pltpu.VMEM((SUBLANES, width), F32),
        ],
        compiler_params=_cparams(("arbitrary", "arbitrary")),
        name="rglru",
    )(proj4, proj4, conv_w, conv_b, wri, br, bi, lam, norm_w)


def _out_route_kernel(yg_ref, yr_ref, x_ref, wog_ref, wor_ref, nw_ref, wr_ref, rb_ref,
                      x1_ref, h2_ref, route_ref, cnt_ref, cnt_scr, *, n_groups, per_group):
    i = pl.program_id(0)
    tm = x_ref.shape[0]

    @pl.when(i == 0)
    def _():
        cnt_scr[...] = jnp.zeros_like(cnt_scr)

    x1 = x_ref[...] + _dot(yg_ref[...], wog_ref[...]) + _dot(yr_ref[...], wor_ref[...])
    x1_ref[...] = x1
    h2 = x1 * lax.rsqrt(jnp.mean(x1 * x1, axis=-1, keepdims=True) + NORM_EPS) * nw_ref[...]
    h2_ref[...] = h2
    h_hi = h2.astype(BF16)
    h_lo = (h2 - h_hi.astype(F32)).astype(BF16)
    hw = _dot(h_hi, wr_ref[...])
    logits = hw[:, :LANES] + hw[:, LANES:] + _dot(h_lo, wr_ref[:, :LANES]) + rb_ref[...]
    lane = lax.broadcasted_iota(jnp.int32, logits.shape, 1)
    neg = jnp.float32(-jnp.inf)
    big = jnp.int32(1 << 20)

    def first_argmax(vals, mask):
        mx = jnp.max(jnp.where(mask, vals, neg), axis=-1, keepdims=True)
        idx = jnp.min(jnp.where(mask & (vals == mx), lane, big), axis=-1, keepdims=True)
        return mx, idx

    gmask = lane < n_groups
    gmax, gidx = first_argmax(logits, gmask)
    zg = jnp.sum(jnp.where(gmask, jnp.exp(logits - gmax), 0.0), axis=-1, keepdims=True)
    grp_p = 1.0 / zg
    lo = n_groups + gidx * per_group
    emask = (lane >= lo) & (lane < lo + per_group)
    m1, i1 = first_argmax(logits, emask)
    m2, i2 = first_argmax(logits, emask & (lane != i1))
    ze = jnp.sum(jnp.where(emask, jnp.exp(logits - m1), 0.0), axis=-1, keepdims=True)
    p1 = 1.0 / ze
    p2 = jnp.exp(m2 - m1) / ze
    den = p1 + p2
    g1 = grp_p * (p1 / den)
    g2 = grp_p * (p2 / den)
    e1 = i1 - n_groups
    e2 = i2 - n_groups
    oh1 = lane == e1
    oh2 = lane == e2
    onehot = jnp.where(oh1 | oh2, 1.0, 0.0)
    r = lax.broadcasted_iota(jnp.int32, (tm, tm), 0)
    c = lax.broadcasted_iota(jnp.int32, (tm, tm), 1)
    prefix = _dot((r > c).astype(BF16), onehot.astype(BF16)) + cnt_scr[...]
    rank1 = jnp.sum(jnp.where(oh1, prefix, 0.0), axis=-1, keepdims=True)
    rank2 = jnp.sum(jnp.where(oh2, prefix, 0.0), axis=-1, keepdims=True)
    cnt_scr[...] = cnt_scr[...] + jnp.sum(onehot, axis=0, keepdims=True)
    cnt_ref[...] = cnt_scr[...]
    out = jnp.where(lane == 0, e1.astype(F32), 0.0)
    out = jnp.where(lane == 1, e2.astype(F32), out)
    out = jnp.where(lane == 2, rank1, out)
    out = jnp.where(lane == 3, rank2, out)
    out = jnp.where(lane == 4, g1, out)
    out = jnp.where(lane == 5, g2, out)
    route_ref[...] = out


def _out_route(y_gdn, y_rg, x2, wo_g, wo_r, norm_w, w_router, b_router, tm, n_groups, per_group):
    n, d = x2.shape
    wg = y_gdn.shape[1]
    wr = y_rg.shape[1]
    kern = functools.partial(_out_route_kernel, n_groups=n_groups, per_group=per_group)
    return pl.pallas_call(
        kern,
        grid=(n // tm,),
        in_specs=[
            pl.BlockSpec((tm, wg), lambda i: (i, 0)),
            pl.BlockSpec((tm, wr), lambda i: (i, 0)),
            pl.BlockSpec((tm, d), lambda i: (i, 0)),
            pl.BlockSpec((wg, d), lambda i: (0, 0), pipeline_mode=pl.Buffered(1)),
            pl.BlockSpec((wr, d), lambda i: (0, 0), pipeline_mode=pl.Buffered(1)),
            pl.BlockSpec((1, d), lambda i: (0, 0)),
            pl.BlockSpec((d, 2 * LANES), lambda i: (0, 0), pipeline_mode=pl.Buffered(1)),
            pl.BlockSpec((1, LANES), lambda i: (0, 0)),
        ],
        out_specs=[
            pl.BlockSpec((tm, d), lambda i: (i, 0)),
            pl.BlockSpec((tm, d), lambda i: (i, 0)),
            pl.BlockSpec((tm, LANES), lambda i: (i, 0)),
            pl.BlockSpec((1, LANES), lambda i: (0, 0)),
        ],
        out_shape=[
            jax.ShapeDtypeStruct((n, d), F32),
            jax.ShapeDtypeStruct((n, d), F32),
            jax.ShapeDtypeStruct((n, LANES), F32),
            jax.ShapeDtypeStruct((1, LANES), F32),
        ],
        scratch_shapes=[pltpu.VMEM((1, LANES), F32)],
        compiler_params=_cparams(("arbitrary",)),
        name="out_route",
    )(y_gdn, y_rg, x2, wo_g, wo_r, norm_w, w_router, b_router)


def _dispatch_kernel(pos1_ref, pos2_ref, pad_ref, h_ref, xs_ref, zeros, sem, zsem, *, n_pad):
    i = pl.program_id(0)
    tm = h_ref.shape[0]
    tz = zeros.shape[0]

    @pl.when(i == 0)
    def _():
        zeros[...] = jnp.zeros_like(zeros)

        def zero_copy(e):
            row = pl.multiple_of(jnp.maximum(pad_ref[e], 0), SUBLANES)
            return pltpu.make_async_copy(zeros, xs_ref.at[pl.ds(row, tz), :], zsem)

        for e in range(n_pad):
            @pl.when(pad_ref[e] >= 0)
            def _():
                zero_copy(e).start()
        for e in range(n_pad):
            @pl.when(pad_ref[e] >= 0)
            def _():
                zero_copy(e).wait()

    def row_copy(r, pos_ref):
        return pltpu.make_async_copy(h_ref.at[pl.ds(r, 1), :],
                                     xs_ref.at[pl.ds(pos_ref[i * tm + r], 1), :], sem)

    def issue(r, carry):
        row_copy(r, pos1_ref).start()
        row_copy(r, pos2_ref).start()
        return carry

    lax.fori_loop(0, tm, issue, 0, unroll=8)
    for _ in range(2):
        pltpu.make_async_copy(h_ref, xs_ref.at[pl.ds(0, tm), :], sem).wait()


def _dispatch(pos1, pos2, pad_row, h2, n_rows, tm, tm_e):
    n, d = h2.shape
    return pl.pallas_call(
        functools.partial(_dispatch_kernel, n_pad=pad_row.shape[0]),
        grid_spec=pltpu.PrefetchScalarGridSpec(
            num_scalar_prefetch=3,
            grid=(n // tm,),
            in_specs=[pl.BlockSpec((tm, d), lambda i, p1, p2, pr: (i, 0))],
            out_specs=pl.BlockSpec(memory_space=pl.ANY),
            scratch_shapes=[pltpu.VMEM((tm_e, d), h2.dtype),
                            pltpu.SemaphoreType.DMA(()), pltpu.SemaphoreType.DMA(())],
        ),
        out_shape=jax.ShapeDtypeStruct((n_rows, d), h2.dtype),
        compiler_params=pltpu.CompilerParams(dimension_semantics=("arbitrary",),
                                             vmem_limit_bytes=VMEM_LIMIT, has_side_effects=True),
        name="dispatch",
    )(pos1, pos2, pad_row, h2)


def _experts_kernel(te_ref, nt_ref, xs_ref, w1_ref, w3_ref, w2_ref, ys_ref, w1b, w3b, w2b):
    j = pl.program_id(0)
    prev = te_ref[jnp.maximum(j - 1, 0)]
    active = j < nt_ref[0]

    @pl.when(active & ((j == 0) | (te_ref[j] != prev)))
    def _():
        w1b[...] = w1_ref[0].astype(BF16)
        w3b[...] = w3_ref[0].astype(BF16)
        w2b[...] = w2_ref[0].astype(BF16)

    @pl.when(active)
    def _():
        x = xs_ref[...].astype(BF16)
        h1 = _dot(x, w1b[...])
        h3 = _dot(x, w3b[...])
        he = (_silu(h1) * h3).astype(BF16)
        ys_ref[...] = _dot(he, w2b[...])

    @pl.when(jnp.logical_not(active))
    def _():
        ys_ref[...] = jnp.zeros_like(ys_ref)


def _experts(tile_expert, n_tiles, xs, w1, w3, w2, tm):
    n_rows, d = xs.shape
    ff = w1.shape[2]

    def wmap(j, te, nt):
        return (te[j], 0, 0)

    def rowmap(j, te, nt):
        return (jnp.minimum(j, nt[0] - 1), 0)

    return pl.pallas_call(
        _experts_kernel,
        grid_spec=pltpu.PrefetchScalarGridSpec(
            num_scalar_prefetch=2,
            grid=(n_rows // tm,),
            in_specs=[
                pl.BlockSpec((tm, d), rowmap),
                pl.BlockSpec((1, d, ff), wmap),
                pl.BlockSpec((1, d, ff), wmap),
                pl.BlockSpec((1, ff, d), wmap),
            ],
            out_specs=pl.BlockSpec((tm, d), lambda j, te, nt: (j, 0)),
            scratch_shapes=[
                pltpu.VMEM((d, ff), BF16),
                pltpu.VMEM((d, ff), BF16),
                pltpu.VMEM((ff, d), BF16),
            ],
        ),
        out_shape=jax.ShapeDtypeStruct((n_rows, d), F32),
        compiler_params=_cparams(("arbitrary",)),
        name="experts",
    )(tile_expert, n_tiles, xs, w1, w3, w2)


def _combine_kernel(pos1_ref, pos2_ref, x1_ref, route_ref, ys_ref, nw_ref, o_ref, y1, y2, sem):
    i = pl.program_id(0)
    tm = x1_ref.shape[0]

    def row_copy(r, pos_ref, dst):
        return pltpu.make_async_copy(ys_ref.at[pl.ds(pos_ref[i * tm + r], 1), :],
                                     dst.at[pl.ds(r, 1), :], sem)

    def issue(r, carry):
        row_copy(r, pos1_ref, y1).start()
        row_copy(r, pos2_ref, y2).start()
        return carry

    lax.fori_loop(0, tm, issue, 0, unroll=8)
    for dst in (y1, y2):
        pltpu.make_async_copy(ys_ref.at[pl.ds(0, tm), :], dst, sem).wait()

    route = route_ref[...]
    g1 = route[:, 4:5]
    g2 = route[:, 5:6]
    x = x1_ref[...] + (g1 * y1[...] + g2 * y2[...])
    o_ref[...] = x * lax.rsqrt(jnp.mean(x * x, axis=-1, keepdims=True) + NORM_EPS) * nw_ref[...]


def _combine(pos1, pos2, x1, route, ys, norm_w, tm):
    n, d = x1.shape
    return pl.pallas_call(
        _combine_kernel,
        grid_spec=pltpu.PrefetchScalarGridSpec(
            num_scalar_prefetch=2,
            grid=(n // tm,),
            in_specs=[
                pl.BlockSpec((tm, d), lambda i, p1, p2: (i, 0)),
                pl.BlockSpec((tm, LANES), lambda i, p1, p2: (i, 0)),
                pl.BlockSpec(memory_space=pl.ANY),
                pl.BlockSpec((1, d), lambda i, p1, p2: (0, 0)),
            ],
            out_specs=pl.BlockSpec((tm, d), lambda i, p1, p2: (i, 0)),
            scratch_shapes=[
                pltpu.VMEM((tm, d), F32),
                pltpu.VMEM((tm, d), F32),
                pltpu.SemaphoreType.DMA(()),
            ],
        ),
        out_shape=jax.ShapeDtypeStruct((n, d), F32),
        compiler_params=_cparams(("arbitrary",)),
        name="combine",
    )(pos1, pos2, x1, route, ys, norm_w)


def _pick(n, pref):
    t = min(n, pref)
    assert n % t == 0, (n, pref)
    return t


def _lane_vec(v, offset=0):
    return jnp.zeros((1, LANES), F32).at[0, offset:offset + v.shape[0]].set(v.astype(F32))


def kernel(x, norm_mix_w, w_in, gdn_conv_w, gdn_a_log, gdn_dt_bias, gdn_norm_w, rg_conv_w, rg_conv_b, rg_w_rgate, rg_b_rgate, rg_w_igate, rg_b_igate, rg_lambda, rg_norm_w, w_out, norm_ffn_w, router_w_group, router_b_group, router_w_expert, router_b_expert, expert_w1, expert_w3, expert_w2, norm_final_w):
    bsz, seq, d = x.shape
    n = bsz * seq
    depth = w_in.shape[0]
    assert depth == 1
    n_heads = gdn_a_log.shape[1]
    gw = n_heads * HEAD_DIM
    rw = rg_lambda.shape[1]
    n_blocks = rg_w_rgate.shape[1]
    n_groups = router_w_group.shape[2]
    n_experts = expert_w1.shape[1]
    per_group = n_experts // n_groups
    assert gw == rw and rw == n_blocks * LANES
    assert n_groups + n_experts <= LANES and 2 * n_heads <= LANES

    x2 = x.reshape(n, d)
    wi = w_in[0]
    nb = 4 * gw
    w_main = jnp.concatenate([wi[:, :nb], wi[:, nb + 2 * n_heads:]], axis=1).astype(BF16)
    w_ba = jnp.pad(wi[:, nb:nb + 2 * n_heads], ((0, 0), (0, LANES - 2 * n_heads))).astype(BF16)
    alog_l = _lane_vec(gdn_a_log[0], n_heads)
    dtb_l = _lane_vec(gdn_dt_bias[0], n_heads)

    tm_in = _pick(n, 512)
    proj, gb = _in_proj(x2, norm_mix_w[0][None, :], w_main, w_ba, alog_l, dtb_l, tm_in, gw, n_heads)
    proj4 = proj.reshape(proj.shape[0], bsz, seq, gw)
    gt = gb[:, :2 * n_heads].T

    lc = _pick(seq, 2 * GDN_CHUNK)
    y_gdn = _gdn(proj4, gb, gt, gdn_conv_w[0], gdn_norm_w[0][None, :], bsz, seq, n_heads, lc)

    wri = jnp.concatenate([rg_w_rgate[0], rg_w_igate[0]], axis=-1).astype(BF16)
    lr = _pick(seq, 256)
    y_rg = _rglru(proj4, rg_conv_w[0], rg_conv_b[0][None, :], wri, rg_b_rgate[0][None, :],
                  rg_b_igate[0][None, :], rg_lambda[0][None, :], rg_norm_w[0][None, :],
                  bsz, seq, n_blocks, lr, 4, 5)

    wo = w_out[0].astype(BF16)
    w_router = jnp.pad(jnp.concatenate([router_w_group[0], router_w_expert[0]], axis=1),
                       ((0, 0), (0, LANES - n_groups - n_experts)))
    w_router_hi = w_router.astype(BF16)
    w_router_lo = (w_router - w_router_hi.astype(F32)).astype(BF16)
    w_router = jnp.concatenate([w_router_hi, w_router_lo], axis=1)
    b_router = _lane_vec(jnp.concatenate([router_b_group[0], router_b_expert[0]]))
    tm_o = _pick(n, 512)
    x1, h2, route, counts = _out_route(y_gdn.reshape(n, gw), y_rg.reshape(n, rw), x2, wo[:gw], wo[gw:],
                                       norm_ffn_w[0][None, :], w_router, b_router, tm_o,
                                       n_groups, per_group)

    tm_e = 256
    cnt = counts[0, :n_experts].astype(jnp.int32)
    tiles_per = (cnt + tm_e - 1) // tm_e
    tile_end = jnp.cumsum(tiles_per)
    tile_start = tile_end - tiles_per
    max_tiles = (2 * n) // tm_e + n_experts
    n_rows = max_tiles * tm_e
    n_tiles = tile_end[-1:].astype(jnp.int32)
    jt = jnp.minimum(jnp.arange(max_tiles, dtype=jnp.int32), n_tiles[0] - 1)
    tile_expert = jnp.sum((tile_end[None, :] <= jt[:, None]).astype(jnp.int32), axis=1)
    row_start = tile_start * tm_e
    tail = n_tiles[0] + jnp.arange(n_experts, dtype=jnp.int32)
    pad_row = jnp.concatenate([jnp.where(tiles_per > 0, (tile_end - 1) * tm_e, -1),
                               jnp.where(tail < max_tiles, tail * tm_e, -1)]).astype(jnp.int32)
    e1 = route[:, 0].astype(jnp.int32)
    e2 = route[:, 1].astype(jnp.int32)
    pos1 = row_start[e1] + route[:, 2].astype(jnp.int32)
    pos2 = row_start[e2] + route[:, 3].astype(jnp.int32)

    tm_d = _pick(n, 512)
    xs = _dispatch(pos1, pos2, pad_row, h2, n_rows, tm_d, tm_e)
    ys = _experts(tile_expert, n_tiles, xs, expert_w1[0], expert_w3[0], expert_w2[0], tm_e)
    out = _combine(pos1, pos2, x1, route, ys, norm_final_w[None, :], tm_d)
    return out.reshape(bsz, seq, d)
```

```python
import functools

import jax
import jax.numpy as jnp
from jax import lax
from jax.experimental import pallas as pl
from jax.experimental.pallas import tpu as pltpu

F32 = jnp.float32
BF16 = jnp.bfloat16
NORM_EPS = 1e-6
LRU_C = 8.0
LANES = 128
SUBLANES = 8
GDN_CHUNK = 64
HEAD_DIM = 128
CONV_W = 4
HALO = 2 * SUBLANES
CONV_ROWS = 128
ROUTE_ROWS = 512
VMEM_LIMIT = 56 * 1024 * 1024
HI = lax.Precision.HIGHEST


def _cparams(sem):
    return pltpu.CompilerParams(dimension_semantics=sem, vmem_limit_bytes=VMEM_LIMIT)


def _softplus(x):
    return jnp.maximum(x, 0.0) + jnp.log1p(jnp.exp(-jnp.abs(x)))


def _sigmoid(x):
    return 0.5 * (jnp.tanh(0.5 * x) + 1.0)


def _shift_matrix():
    r = lax.broadcasted_iota(jnp.int32, ((CONV_W - 1) * CONV_ROWS, CONV_ROWS + HALO), 0)
    c = lax.broadcasted_iota(jnp.int32, ((CONV_W - 1) * CONV_ROWS, CONV_ROWS + HALO), 1)
    return (c == (r % CONV_ROWS) + (r // CONV_ROWS) + HALO - (CONV_W - 1)).astype(BF16)


def _causal_conv(shift, xwin, w_ref, lanes):
    sh = _dot(shift, xwin)
    acc = w_ref[CONV_W - 1:CONV_W, lanes] * xwin[HALO:, :].astype(F32)
    for j in range(CONV_W - 1):
        acc = acc + w_ref[j:j + 1, lanes] * sh[j * CONV_ROWS:(j + 1) * CONV_ROWS]
    return acc


def _silu_half(hx):
    return hx + hx * jnp.tanh(hx)


def _silu(x):
    return _silu_half(0.5 * x)


def _dot(a, b, precision=None):
    return jnp.dot(a, b, preferred_element_type=F32, precision=precision)


def _dot_nt(a, b, precision=None):
    return lax.dot_general(a, b, (((1,), (1,)), ((), ())), preferred_element_type=F32,
                           precision=precision)


def _in_proj_kernel(x_ref, nw_ref, w_ref, wba_ref, alog_ref, dtb_ref, o_ref, gb_ref, gt_ref, h_scr,
                    *, n_heads):
    j = pl.program_id(1)

    @pl.when(j == 0)
    def _():
        x = x_ref[...]
        ms = jnp.mean(x * x, axis=-1, keepdims=True)
        h = (x * lax.rsqrt(ms + NORM_EPS) * nw_ref[...]).astype(BF16)
        h_scr[...] = h
        ba = _dot(h, wba_ref[...])
        lane = lax.broadcasted_iota(jnp.int32, ba.shape, 1)
        beta = _sigmoid(ba)
        g = -jnp.exp(alog_ref[...]) * _softplus(ba + dtb_ref[...])
        gb = jnp.where(lane < n_heads, beta, g)
        gb_ref[...] = gb
        gt_ref[...] = gb.T[:2 * n_heads, :]

    o_ref[0] = _dot(h_scr[...], w_ref[...]).astype(o_ref.dtype)


def _in_proj(x2, norm_w, w_main, w_ba, alog_l, dtb_l, tm, tn, n_heads):
    n, d = x2.shape
    npieces = w_main.shape[1] // tn
    return pl.pallas_call(
        functools.partial(_in_proj_kernel, n_heads=n_heads),
        grid=(n // tm, npieces),
        in_specs=[
            pl.BlockSpec((tm, d), lambda i, j: (i, 0)),
            pl.BlockSpec((1, d), lambda i, j: (0, 0)),
            pl.BlockSpec((d, tn), lambda i, j: (0, j)),
            pl.BlockSpec((d, LANES), lambda i, j: (0, 0)),
            pl.BlockSpec((1, LANES), lambda i, j: (0, 0)),
            pl.BlockSpec((1, LANES), lambda i, j: (0, 0)),
        ],
        out_specs=[
            pl.BlockSpec((1, tm, tn), lambda i, j: (j, i, 0)),
            pl.BlockSpec((tm, LANES), lambda i, j: (i, 0)),
            pl.BlockSpec((2 * n_heads, tm), lambda i, j: (0, i)),
        ],
        out_shape=[
            jax.ShapeDtypeStruct((npieces, n, tn), BF16),
            jax.ShapeDtypeStruct((n, LANES), F32),
            jax.ShapeDtypeStruct((2 * n_heads, n), F32),
        ],
        scratch_shapes=[pltpu.VMEM((tm, d), BF16)],
        compiler_params=_cparams(("arbitrary", "arbitrary")),
        name="in_proj",
    )(x2, norm_w, w_main, w_ba, alog_l, dtb_l)


def _gdn_kernel(q_ref, k_ref, v_ref, z_ref, gb_ref, gt_ref, cw_ref, nw_ref, o_ref,
                xbuf, qkv, state, *, n_heads, lc):
    t = pl.program_id(1)
    hd = HEAD_DIM
    width = n_heads * hd
    c = GDN_CHUNK

    @pl.when(t == 0)
    def _():
        xbuf[0:HALO, :] = jnp.zeros((HALO, 3 * width), BF16)
        state[...] = jnp.zeros_like(state)

    xbuf[HALO:HALO + lc, 0:width] = q_ref[0, 0]
    xbuf[HALO:HALO + lc, width:2 * width] = k_ref[0, 0]
    xbuf[HALO:HALO + lc, 2 * width:3 * width] = v_ref[0, 0]

    row = lax.broadcasted_iota(jnp.int32, (c, c), 0)
    col = lax.broadcasted_iota(jnp.int32, (c, c), 1)
    causal = row >= col
    strict = row > col
    eye = (row == col).astype(F32)
    ltri = causal.astype(F32)
    r2 = lax.broadcasted_iota(jnp.int32, (lc, lc), 0)
    c2 = lax.broadcasted_iota(jnp.int32, (lc, lc), 1)
    utri = ((r2 <= c2) & ((r2 // c) == (c2 // c))).astype(F32)
    grow_all = _dot(gt_ref[...], utri, HI)

    def l2n(x, scale=None):
        inv = lax.rsqrt(jnp.sum(x * x, axis=-1, keepdims=True) + NORM_EPS)
        return x * (inv if scale is None else inv * scale)

    nci = lc // c
    gcols = [_dot(ltri, gb_ref[ci * c:(ci + 1) * c, :], HI) for ci in range(nci)]

    shift = _shift_matrix()
    pair = 2 * hd
    for piece in range(3):
        for hp in range(width // pair):
            lanes = slice(piece * width + hp * pair, piece * width + (hp + 1) * pair)
            for sub in range(lc // CONV_ROWS):
                rows = slice(sub * CONV_ROWS, (sub + 1) * CONV_ROWS)
                y = _silu_half(_causal_conv(shift, xbuf[sub * CONV_ROWS:(sub + 1) * CONV_ROWS + HALO, lanes],
                                            cw_ref, lanes))
                for half in range(2):
                    yh = y[:, half * hd:(half + 1) * hd]
                    if piece == 0:
                        yh = l2n(yh, hd ** -0.5)
                    elif piece == 1:
                        yh = l2n(yh)
                    qkv[rows, lanes.start + half * hd:lanes.start + (half + 1) * hd] = yh
    xbuf[0:HALO, :] = xbuf[lc:lc + HALO, :]

    units = [(ci, h) for ci in range(nci) for h in range(n_heads)]
    blk = SUBLANES * 2
    diag_blk = (row // blk) == (col // blk)
    bf = lambda v: v.astype(BF16)

    st = {}
    for (ci, h) in units:
        sl = slice(ci * c, (ci + 1) * c)
        qn = qkv[sl, h * hd:(h + 1) * hd]
        kn = qkv[sl, width + h * hd:width + (h + 1) * hd]
        vv = qkv[sl, 2 * width + h * hd:2 * width + (h + 1) * hd]
        beta_b = jnp.broadcast_to(gb_ref[sl, h:h + 1], (c, hd))
        gc_b = jnp.broadcast_to(gcols[ci][:, n_heads + h:n_heads + h + 1], (c, hd))
        gc_row = grow_all[n_heads + h:n_heads + h + 1, sl]
        eg = jnp.exp(gc_b)
        qkk = _dot_nt(bf(jnp.concatenate([qn, kn], axis=0)), bf(kn))
        dmat = gc_b[:, :c] - gc_row
        decay = jnp.where(causal, jnp.exp(jnp.minimum(dmat, 0.0)), 0.0)
        a = jnp.where(strict, beta_b[:, :c] * qkk[c:] * decay, 0.0)
        dm = jnp.where(diag_blk, a, 0.0)
        st[(ci, h)] = dict(
            attn=bf(qkk[:c] * decay), dm=dm, lm=bf(a - dm),
            rhs=bf(jnp.concatenate([vv * beta_b, kn * (beta_b * eg)], axis=1)),
            qg=qn * eg,
            kdt=bf(kn.T * jnp.exp(gc_row[:, c - 1:c] - gc_row)),
            sdec=jnp.exp(gc_b[c - 1:c, :]))

    for u in units:
        d = st[u]
        d["x"] = _dot(bf(d["dm"]), bf(d["dm"]))
        d["p"] = eye - d["dm"]
    nsq = blk.bit_length() - 2
    for it in range(nsq):
        for u in units:
            d = st[u]
            if it < nsq - 1:
                px = _dot(bf(jnp.concatenate([d["p"], d["x"]], axis=0)), bf(d["x"]))
                d["p"] = d["p"] + px[:c]
                d["x"] = px[c:]
            else:
                d["p"] = d["p"] + _dot(bf(d["p"]), bf(d["x"]))
    for u in units:
        d = st[u]
        pb = bf(d["p"])
        d["m"] = _dot(pb, d["lm"])
        d["y"] = _dot(pb, d["rhs"])
    npow = c // blk - 1
    for u in units:
        d = st[u]
        d["tp"] = eye - d["m"]
        d["mk"] = d["m"]
    for k in range(2, npow + 1):
        for u in units:
            d = st[u]
            d["mk"] = _dot(bf(d["m"]), bf(d["mk"]))
            d["tp"] = d["tp"] + d["mk"] if k % 2 == 0 else d["tp"] - d["mk"]
    for u in units:
        d = st[u]
        d["sol"] = _dot(bf(d["tp"]), bf(d["y"]))

    states = [state[h] for h in range(n_heads)]
    for ci in range(nci):
        sl = slice(ci * c, (ci + 1) * c)
        ws = {}
        for h in range(n_heads):
            d = st[(ci, h)]
            wq = bf(jnp.concatenate([d["sol"][:, hd:], d["qg"]], axis=0))
            ws[h] = _dot(wq, bf(states[h]))
        upd = {}
        vnew = {}
        for h in range(n_heads):
            d = st[(ci, h)]
            vnew[h] = bf(d["sol"][:, :hd] - ws[h][:c])
            upd[h] = _dot(jnp.concatenate([d["attn"], d["kdt"]], axis=0), vnew[h])
        for h in range(n_heads):
            d = st[(ci, h)]
            o = ws[h][c:] + upd[h][:c]
            states[h] = states[h] * d["sdec"] + upd[h][c:]
            on = o * lax.rsqrt(jnp.mean(o * o, axis=-1, keepdims=True) + NORM_EPS) * nw_ref[...]
            zz = z_ref[0, 0, sl, h * hd:(h + 1) * hd].astype(F32)
            o_ref[0, sl, h * hd:(h + 1) * hd] = (on * _silu(zz)).astype(o_ref.dtype)
    for h in range(n_heads):
        state[h] = states[h]


def _gdn(proj4, gb, gt, conv_w, norm_w, bsz, seq, n_heads, lc):
    width = n_heads * HEAD_DIM
    nt = seq // lc
    kern = functools.partial(_gdn_kernel, n_heads=n_heads, lc=lc)

    def piece(p):
        return pl.BlockSpec((1, 1, lc, width), lambda b, t, p=p: (p, b, t, 0))

    return pl.pallas_call(
        kern,
        grid=(bsz, nt),
        in_specs=[
            piece(0), piece(1), piece(2), piece(3),
            pl.BlockSpec((lc, LANES), lambda b, t: (b * nt + t, 0)),
            pl.BlockSpec((2 * n_heads, lc), lambda b, t: (0, b * nt + t)),
            pl.BlockSpec((CONV_W, 3 * width), lambda b, t: (0, 0)),
            pl.BlockSpec((1, HEAD_DIM), lambda b, t: (0, 0)),
        ],
        out_specs=pl.BlockSpec((1, lc, width), lambda b, t: (b, t, 0)),
        out_shape=jax.ShapeDtypeStruct((bsz, seq, width), BF16),
        scratch_shapes=[
            pltpu.VMEM((lc + HALO, 3 * width), BF16),
            pltpu.VMEM((lc, 3 * width), F32),
            pltpu.VMEM((n_heads, HEAD_DIM, HEAD_DIM), F32),
        ],
        compiler_params=_cparams(("arbitrary", "arbitrary")),
        name="gdn",
    )(proj4, proj4, proj4, proj4, gb, gt, conv_w, norm_w)


def _rglru_kernel(x_ref, g_ref, cw_ref, cb_ref, wri_ref, br_ref, bi_ref, lam_ref, nw_ref, o_ref,
                  xbuf, a_scr, b_scr, hlast, *, n_blocks, lr):
    t = pl.program_id(1)
    bd = LANES
    width = n_blocks * bd

    @pl.when(t == 0)
    def _():
        xbuf[0:HALO, :] = jnp.zeros((HALO, width), BF16)
        hlast[...] = jnp.zeros_like(hlast)

    xbuf[HALO:HALO + lr, :] = x_ref[0, 0]
    shift = _shift_matrix()
    pair = 2 * bd
    xc_pairs = []
    for p in range(width // pair):
        lanes = slice(p * pair, (p + 1) * pair)
        xc_pairs.append(jnp.concatenate(
            [_causal_conv(shift, xbuf[s0:s0 + CONV_ROWS + HALO, lanes], cw_ref, lanes)
             for s0 in range(0, lr, CONV_ROWS)], axis=0) + cb_ref[:, lanes])
    xbuf[0:HALO, :] = xbuf[lr:lr + HALO, :]

    sp_lam = _softplus(-lam_ref[...])
    sub = lax.broadcasted_iota(jnp.int32, (lr, bd), 0) % SUBLANES
    for n in range(n_blocks):
        sl = slice(n * bd, (n + 1) * bd)
        xb = xc_pairs[n // 2][:, (n % 2) * bd:(n % 2 + 1) * bd]
        gates = _dot(xb.astype(BF16), wri_ref[n])
        r = _sigmoid(gates[:, :bd] + br_ref[:, sl])
        i = _sigmoid(gates[:, bd:] + bi_ref[:, sl])
        log_a = -LRU_C * r * sp_lam[:, sl]
        a = jnp.exp(log_a)
        th = jnp.tanh(log_a)
        b = jnp.sqrt(-2.0 * th / (1.0 - th)) * (i * xb)
        for s in (1, 2, 4):
            a_s = pltpu.roll(a, s, axis=0)
            b_s = pltpu.roll(b, s, axis=0)
            m = sub >= s
            b = jnp.where(m, a * b_s + b, b)
            a = jnp.where(m, a * a_s, a)
        a_scr[:, sl] = a
        b_scr[:, sl] = b

    def tile_step(i, hprev):
        r0 = pl.multiple_of(i * SUBLANES, SUBLANES)
        hh = a_scr[pl.ds(r0, SUBLANES), :] * hprev + b_scr[pl.ds(r0, SUBLANES), :]
        b_scr[pl.ds(r0, SUBLANES), :] = hh
        return jnp.broadcast_to(hh[SUBLANES - 1:SUBLANES, :], (SUBLANES, width))

    hfin = lax.fori_loop(0, lr // SUBLANES, tile_step, hlast[...])
    hlast[...] = hfin

    for n in range(n_blocks):
        sl = slice(n * bd, (n + 1) * bd)
        gt = g_ref[0, 0, :, sl].astype(F32)
        y = b_scr[:, sl] * jax.nn.gelu(gt, approximate=True)
        yn = y * lax.rsqrt(jnp.mean(y * y, axis=-1, keepdims=True) + NORM_EPS) * nw_ref[:, sl]
        o_ref[0, :, sl] = yn.astype(o_ref.dtype)


def _rglru(proj4, conv_w, conv_b, wri, br, bi, lam, norm_w, bsz, seq, n_blocks, lr, px, pg):
    width = n_blocks * LANES
    nt = seq // lr
    kern = functools.partial(_rglru_kernel, n_blocks=n_blocks, lr=lr)

    def vec():
        return pl.BlockSpec((1, width), lambda b, t: (0, 0))

    return pl.pallas_call(
        kern,
        grid=(bsz, nt),
        in_specs=[
            pl.BlockSpec((1, 1, lr, width), lambda b, t: (px, b, t, 0)),
            pl.BlockSpec((1, 1, lr, width), lambda b, t: (pg, b, t, 0)),
            pl.BlockSpec((CONV_W, width), lambda b, t: (0, 0)),
            vec(),
            pl.BlockSpec((n_blocks, LANES, 2 * LANES), lambda b, t: (0, 0, 0)),
            vec(), vec(), vec(), vec(),
        ],
        out_specs=pl.BlockSpec((1, lr, width), lambda b, t: (b, t, 0)),
        out_shape=jax.ShapeDtypeStruct((bsz, seq, width), BF16),
        scratch_shapes=[
            pltpu.VMEM((lr + HALO, width), BF16),
            pltpu.VMEM((lr, width), F32),
            pltpu.VMEM((lr, width), F32),
            pltpu.VMEM((SUBLANES, width), F32),
        ],
        compiler_params=_cparams(("arbitrary", "arbitrary")),
        name="rglru",
    )(proj4, proj4, conv_w, conv_b, wri, br, bi, lam, norm_w)


def _out_route_kernel(yg_ref, yr_ref, x_ref, wog_ref, wor_ref, nw_ref, wr_ref, rb_ref,
                      x1_ref, h2_ref, route_ref, routet_ref, cnt_ref, cnt_scr, *, n_groups, per_group):
    i = pl.program_id(0)
    tm = x_ref.shape[0]
    ts = min(tm, ROUTE_ROWS)

    @pl.when(i == 0)
    def _():
        cnt_scr[...] = jnp.zeros_like(cnt_scr)

    lane = lax.broadcasted_iota(jnp.int32, (ts, LANES), 1)
    neg = jnp.float32(-jnp.inf)
    big = jnp.int32(1 << 20)
    r = lax.broadcasted_iota(jnp.int32, (ts, ts), 0)
    c = lax.broadcasted_iota(jnp.int32, (ts, ts), 1)
    before = (r > c).astype(BF16)

    def first_argmax(vals, mask):
        mx = jnp.max(jnp.where(mask, vals, neg), axis=-1, keepdims=True)
        idx = jnp.min(jnp.where(mask & (vals == mx), lane, big), axis=-1, keepdims=True)
        return mx, idx

    counts = cnt_scr[...]
    for s0 in range(0, tm, ts):
        rows = slice(s0, s0 + ts)
        x1 = x_ref[rows, :] + _dot(yg_ref[rows, :], wog_ref[...]) + _dot(yr_ref[rows, :], wor_ref[...])
        x1_ref[rows, :] = x1
        h2 = x1 * lax.rsqrt(jnp.mean(x1 * x1, axis=-1, keepdims=True) + NORM_EPS) * nw_ref[...]
        h2_ref[rows, :] = h2
        h_hi = h2.astype(BF16)
        h_lo = (h2 - h_hi.astype(F32)).astype(BF16)
        hw = _dot(h_hi, wr_ref[...])
        logits = hw[:, :LANES] + hw[:, LANES:] + _dot(h_lo, wr_ref[:, :LANES]) + rb_ref[...]
        gmask = lane < n_groups
        gmax, gidx = first_argmax(logits, gmask)
        zg = jnp.sum(jnp.where(gmask, jnp.exp(logits - gmax), 0.0), axis=-1, keepdims=True)
        grp_p = 1.0 / zg
        lo = n_groups + gidx * per_group
        emask = (lane >= lo) & (lane < lo + per_group)
        m1, i1 = first_argmax(logits, emask)
        m2, i2 = first_argmax(logits, emask & (lane != i1))
        ze = jnp.sum(jnp.where(emask, jnp.exp(logits - m1), 0.0), axis=-1, keepdims=True)
        p1 = 1.0 / ze
        p2 = jnp.exp(m2 - m1) / ze
        den = p1 + p2
        g1 = grp_p * (p1 / den)
        g2 = grp_p * (p2 / den)
        e1 = i1 - n_groups
        e2 = i2 - n_groups
        oh1 = lane == e1
        oh2 = lane == e2
        onehot = jnp.where(oh1 | oh2, 1.0, 0.0)
        prefix = _dot(before, onehot.astype(BF16)) + counts
        rank1 = jnp.sum(jnp.where(oh1, prefix, 0.0), axis=-1, keepdims=True)
        rank2 = jnp.sum(jnp.where(oh2, prefix, 0.0), axis=-1, keepdims=True)
        counts = counts + jnp.sum(onehot, axis=0, keepdims=True)
        out = jnp.where(lane == 0, e1.astype(F32), 0.0)
        out = jnp.where(lane == 1, e2.astype(F32), out)
        out = jnp.where(lane == 2, rank1, out)
        out = jnp.where(lane == 3, rank2, out)
        out = jnp.where(lane == 4, g1, out)
        out = jnp.where(lane == 5, g2, out)
        route_ref[rows, :] = out
        routet_ref[:, rows] = out.T[:SUBLANES, :]
    cnt_scr[...] = counts
    cnt_ref[...] = counts


def _out_route(y_gdn, y_rg, x2, wo_g, wo_r, norm_w, w_router, b_router, tm, n_groups, per_group):
    n, d = x2.shape
    wg = y_gdn.shape[1]
    wr = y_rg.shape[1]
    kern = functools.partial(_out_route_kernel, n_groups=n_groups, per_group=per_group)
    return pl.pallas_call(
        kern,
        grid=(n // tm,),
        in_specs=[
            pl.BlockSpec((tm, wg), lambda i: (i, 0)),
            pl.BlockSpec((tm, wr), lambda i: (i, 0)),
            pl.BlockSpec((tm, d), lambda i: (i, 0)),
            pl.BlockSpec((wg, d), lambda i: (0, 0), pipeline_mode=pl.Buffered(1)),
            pl.BlockSpec((wr, d), lambda i: (0, 0), pipeline_mode=pl.Buffered(1)),
            pl.BlockSpec((1, d), lambda i: (0, 0)),
            pl.BlockSpec((d, 2 * LANES), lambda i: (0, 0), pipeline_mode=pl.Buffered(1)),
            pl.BlockSpec((1, LANES), lambda i: (0, 0)),
        ],
        out_specs=[
            pl.BlockSpec((tm, d), lambda i: (i, 0)),
            pl.BlockSpec((tm, d), lambda i: (i, 0)),
            pl.BlockSpec((tm, LANES), lambda i: (i, 0)),
            pl.BlockSpec((SUBLANES, tm), lambda i: (0, i)),
            pl.BlockSpec((1, LANES), lambda i: (0, 0)),
        ],
        out_shape=[
            jax.ShapeDtypeStruct((n, d), F32),
            jax.ShapeDtypeStruct((n, d), F32),
            jax.ShapeDtypeStruct((n, LANES), F32),
            jax.ShapeDtypeStruct((SUBLANES, n), F32),
            jax.ShapeDtypeStruct((1, LANES), F32),
        ],
        scratch_shapes=[pltpu.VMEM((1, LANES), F32)],
        compiler_params=_cparams(("arbitrary",)),
        name="out_route",
    )(y_gdn, y_rg, x2, wo_g, wo_r, norm_w, w_router, b_router)


def _dispatch_kernel(pos1_ref, pos2_ref, pad_ref, h_ref, xs_ref, zeros, sem, zsem, *, n_pad):
    i = pl.program_id(0)
    tm = h_ref.shape[0]
    tz = zeros.shape[0]

    @pl.when(i == 0)
    def _():
        zeros[...] = jnp.zeros_like(zeros)

        def zero_copy(e):
            row = pl.multiple_of(jnp.maximum(pad_ref[e], 0), SUBLANES)
            return pltpu.make_async_copy(zeros, xs_ref.at[pl.ds(row, tz), :], zsem)

        for e in range(n_pad):
            @pl.when(pad_ref[e] >= 0)
            def _():
                zero_copy(e).start()
        for e in range(n_pad):
            @pl.when(pad_ref[e] >= 0)
            def _():
                zero_copy(e).wait()

    def row_copy(r, pos_ref):
        return pltpu.make_async_copy(h_ref.at[pl.ds(r, 1), :],
                                     xs_ref.at[pl.ds(pos_ref[i * tm + r], 1), :], sem)

    def issue(r, carry):
        row_copy(r, pos1_ref).start()
        row_copy(r, pos2_ref).start()
        return carry

    lax.fori_loop(0, tm, issue, 0, unroll=8)
    for _ in range(2):
        pltpu.make_async_copy(h_ref, xs_ref.at[pl.ds(0, tm), :], sem).wait()


def _dispatch(pos1, pos2, pad_row, h2, n_rows, tm, tm_e):
    n, d = h2.shape
    return pl.pallas_call(
        functools.partial(_dispatch_kernel, n_pad=pad_row.shape[0]),
        grid_spec=pltpu.PrefetchScalarGridSpec(
            num_scalar_prefetch=3,
            grid=(n // tm,),
            in_specs=[pl.BlockSpec((tm, d), lambda i, p1, p2, pr: (i, 0))],
            out_specs=pl.BlockSpec(memory_space=pl.ANY),
            scratch_shapes=[pltpu.VMEM((tm_e, d), h2.dtype),
                            pltpu.SemaphoreType.DMA(()), pltpu.SemaphoreType.DMA(())],
        ),
        out_shape=jax.ShapeDtypeStruct((n_rows, d), h2.dtype),
        compiler_params=pltpu.CompilerParams(dimension_semantics=("arbitrary",),
                                             vmem_limit_bytes=VMEM_LIMIT, has_side_effects=True),
        name="dispatch",
    )(pos1, pos2, pad_row, h2)


def _experts_kernel(te_ref, nt_ref, slot_ref, next_ref, xs_ref, w1_ref, w3_ref, w2_ref, ys_ref,
                    w1f, w3f, w2f, w1b, w3b, w2b, sem):
    j = pl.program_id(0)
    prev = te_ref[jnp.maximum(j - 1, 0)]
    active = j < nt_ref[0]

    def weight_copies(e, buf):
        return [pltpu.make_async_copy(src.at[e], dst.at[buf], sem.at[buf])
                for src, dst in ((w1_ref, w1f), (w3_ref, w3f), (w2_ref, w2f))]

    @pl.when(active & ((j == 0) | (te_ref[j] != prev)))
    def _():
        buf = slot_ref[j]

        @pl.when(j == 0)
        def _():
            for cp in weight_copies(te_ref[0], 0):
                cp.start()

        for cp in weight_copies(te_ref[j], buf):
            cp.wait()

        @pl.when(next_ref[j] >= 0)
        def _():
            for cp in weight_copies(next_ref[j], 1 - buf):
                cp.start()

        w1b[...] = w1f[buf].astype(BF16)
        w3b[...] = w3f[buf].astype(BF16)
        w2b[...] = w2f[buf].astype(BF16)

    @pl.when(active)
    def _():
        x = xs_ref[...].astype(BF16)
        h1 = _dot(x, w1b[...])
        h3 = _dot(x, w3b[...])
        he = (_silu(h1) * h3).astype(BF16)
        ys_ref[...] = _dot(he, w2b[...])

    @pl.when(jnp.logical_not(active))
    def _():
        ys_ref[...] = jnp.zeros_like(ys_ref)


def _experts(tile_expert, n_tiles, tile_slot, tile_next, xs, w1, w3, w2, tm):
    n_rows, dp = xs.shape
    d, ff = w1.shape[1], w1.shape[2]

    def rowmap(j, te, nt, sl, nx):
        return (jnp.minimum(j, nt[0] - 1), 0)

    return pl.pallas_call(
        _experts_kernel,
        grid_spec=pltpu.PrefetchScalarGridSpec(
            num_scalar_prefetch=4,
            grid=(n_rows // tm,),
            in_specs=[
                pl.BlockSpec((tm, dp), rowmap),
                pl.BlockSpec(memory_space=pl.ANY),
                pl.BlockSpec(memory_space=pl.ANY),
                pl.BlockSpec(memory_space=pl.ANY),
            ],
            out_specs=pl.BlockSpec((tm, dp), lambda j, te, nt, sl, nx: (j, 0)),
            scratch_shapes=[
                pltpu.VMEM((2, d, ff), F32),
                pltpu.VMEM((2, d, ff), F32),
                pltpu.VMEM((2, ff, d), F32),
                pltpu.VMEM((d, ff), BF16),
                pltpu.VMEM((d, ff), BF16),
                pltpu.VMEM((ff, d), BF16),
                pltpu.SemaphoreType.DMA((2,)),
            ],
        ),
        out_shape=jax.ShapeDtypeStruct((n_rows, dp), F32),
        compiler_params=_cparams(("arbitrary",)),
        name="experts",
    )(tile_expert, n_tiles, tile_slot, tile_next, xs, w1, w3, w2)


def _combine_kernel(pos1_ref, pos2_ref, x1_ref, route_ref, ys_ref, nw_ref, o_ref, y1, y2, sem):
    i = pl.program_id(0)
    tm = x1_ref.shape[0]
    slot = i % 2

    def start_tile(tile, buf):
        def issue(r, carry):
            for pos_ref, dst in ((pos1_ref, y1), (pos2_ref, y2)):
                pltpu.make_async_copy(ys_ref.at[pl.ds(pos_ref[tile * tm + r], 1), :],
                                      dst.at[buf, pl.ds(r, 1), :], sem.at[buf]).start()
            return carry

        lax.fori_loop(0, tm, issue, 0, unroll=8)

    @pl.when(i == 0)
    def _():
        start_tile(0, 0)

    @pl.when(i + 1 < pl.num_programs(0))
    def _():
        start_tile(i + 1, 1 - slot)

    for dst in (y1, y2):
        pltpu.make_async_copy(ys_ref.at[pl.ds(0, tm), :], dst.at[slot], sem.at[slot]).wait()

    route = route_ref[...]
    g1 = route[:, 4:5]
    g2 = route[:, 5:6]
    x = x1_ref[...] + (g1 * y1[slot] + g2 * y2[slot])
    o_ref[...] = x * lax.rsqrt(jnp.mean(x * x, axis=-1, keepdims=True) + NORM_EPS) * nw_ref[...]


def _combine(pos1, pos2, x1, route, ys, norm_w, tm):
    n, d = x1.shape
    return pl.pallas_call(
        _combine_kernel,
        grid_spec=pltpu.PrefetchScalarGridSpec(
            num_scalar_prefetch=2,
            grid=(n // tm,),
            in_specs=[
                pl.BlockSpec((tm, d), lambda i, p1, p2: (i, 0)),
                pl.BlockSpec((tm, LANES), lambda i, p1, p2: (i, 0)),
                pl.BlockSpec(memory_space=pl.ANY),
                pl.BlockSpec((1, d), lambda i, p1, p2: (0, 0)),
            ],
            out_specs=pl.BlockSpec((tm, d), lambda i, p1, p2: (i, 0)),
            scratch_shapes=[
                pltpu.VMEM((2, tm, ys.shape[1]), ys.dtype),
                pltpu.VMEM((2, tm, ys.shape[1]), ys.dtype),
                pltpu.SemaphoreType.DMA((2,)),
            ],
        ),
        out_shape=jax.ShapeDtypeStruct((n, d), F32),
        compiler_params=_cparams(("arbitrary",)),
        name="combine",
    )(pos1, pos2, x1, route, ys, norm_w)


def _pick(n, pref):
    t = min(n, pref)
    assert n % t == 0, (n, pref)
    return t


def _lane_vec(v, offset=0):
    return jnp.zeros((1, LANES), F32).at[0, offset:offset + v.shape[0]].set(v.astype(F32))


def kernel(x, norm_mix_w, w_in, gdn_conv_w, gdn_a_log, gdn_dt_bias, gdn_norm_w, rg_conv_w, rg_conv_b, rg_w_rgate, rg_b_rgate, rg_w_igate, rg_b_igate, rg_lambda, rg_norm_w, w_out, norm_ffn_w, router_w_group, router_b_group, router_w_expert, router_b_expert, expert_w1, expert_w3, expert_w2, norm_final_w):
    bsz, seq, d = x.shape
    n = bsz * seq
    depth = w_in.shape[0]
    assert depth == 1
    n_heads = gdn_a_log.shape[1]
    gw = n_heads * HEAD_DIM
    rw = rg_lambda.shape[1]
    n_blocks = rg_w_rgate.shape[1]
    n_groups = router_w_group.shape[2]
    n_experts = expert_w1.shape[1]
    per_group = n_experts // n_groups
    assert gw == rw and rw == n_blocks * LANES
    assert n_groups + n_experts <= LANES and 2 * n_heads <= LANES

    x2 = x.reshape(n, d)
    wi = w_in[0]
    nb = 4 * gw
    w_main = jnp.concatenate([wi[:, :nb], wi[:, nb + 2 * n_heads:]], axis=1).astype(BF16)
    w_ba = jnp.pad(wi[:, nb:nb + 2 * n_heads], ((0, 0), (0, LANES - 2 * n_heads))).astype(BF16)
    alog_l = _lane_vec(gdn_a_log[0], n_heads)
    dtb_l = _lane_vec(gdn_dt_bias[0], n_heads)

    tm_in = _pick(n, 1024)
    proj, gb, gt = _in_proj(x2, norm_mix_w[0][None, :], w_main, w_ba, alog_l, dtb_l, tm_in, gw, n_heads)
    proj4 = proj.reshape(proj.shape[0], bsz, seq, gw)

    lc = _pick(seq, 2 * GDN_CHUNK)
    y_gdn = _gdn(proj4, gb, gt, 0.5 * gdn_conv_w[0], gdn_norm_w[0][None, :], bsz, seq, n_heads, lc)

    wri = jnp.concatenate([rg_w_rgate[0], rg_w_igate[0]], axis=-1).astype(BF16)
    lr = _pick(seq, 256)
    y_rg = _rglru(proj4, rg_conv_w[0], rg_conv_b[0][None, :], wri, rg_b_rgate[0][None, :],
                  rg_b_igate[0][None, :], rg_lambda[0][None, :], rg_norm_w[0][None, :],
                  bsz, seq, n_blocks, lr, 4, 5)

    wo = w_out[0].astype(BF16)
    w_router = jnp.pad(jnp.concatenate([router_w_group[0], router_w_expert[0]], axis=1),
                       ((0, 0), (0, LANES - n_groups - n_experts)))
    w_router_hi = w_router.astype(BF16)
    w_router_lo = (w_router - w_router_hi.astype(F32)).astype(BF16)
    w_router = jnp.concatenate([w_router_hi, w_router_lo], axis=1)
    b_router = _lane_vec(jnp.concatenate([router_b_group[0], router_b_expert[0]]))
    tm_o = _pick(n, 512)
    x1, h2, route, route_t, counts = _out_route(y_gdn.reshape(n, gw), y_rg.reshape(n, rw), x2, wo[:gw], wo[gw:],
                                       norm_ffn_w[0][None, :], w_router, b_router, tm_o,
                                       n_groups, per_group)

    tm_e = 256
    cnt = counts[0, :n_experts].astype(jnp.int32)
    tiles_per = (cnt + tm_e - 1) // tm_e
    tile_end = jnp.cumsum(tiles_per)
    tile_start = tile_end - tiles_per
    max_tiles = (2 * n) // tm_e + n_experts
    n_rows = max_tiles * tm_e
    n_tiles = tile_end[-1:].astype(jnp.int32)
    jt = jnp.minimum(jnp.arange(max_tiles, dtype=jnp.int32), n_tiles[0] - 1)
    tile_expert = jnp.sum((tile_end[None, :] <= jt[:, None]).astype(jnp.int32), axis=1)
    row_start = tile_start * tm_e
    has_tiles = tiles_per > 0
    eidx = jnp.arange(n_experts, dtype=jnp.int32)
    later = jnp.where(has_tiles[None, :] & (eidx[None, :] > eidx[:, None]), eidx[None, :], n_experts)
    next_expert = jnp.min(later, axis=1)
    next_expert = jnp.where(next_expert < n_experts, next_expert, -1).astype(jnp.int32)
    expert_slot = ((jnp.cumsum(has_tiles.astype(jnp.int32)) - 1) % 2).astype(jnp.int32)
    tile_slot = expert_slot[tile_expert]
    tile_next = next_expert[tile_expert]
    tail = n_tiles[0] + jnp.arange(n_experts, dtype=jnp.int32)
    pad_row = jnp.concatenate([jnp.where(tiles_per > 0, (tile_end - 1) * tm_e, -1),
                               jnp.where(tail < max_tiles, tail * tm_e, -1)]).astype(jnp.int32)
    ri = route_t[:4].astype(jnp.int32)
    pos1 = row_start[ri[0]] + ri[2]
    pos2 = row_start[ri[1]] + ri[3]

    tm_d = _pick(n, 512)
    xs = _dispatch(pos1, pos2, pad_row, h2, n_rows, tm_d, tm_e)
    ys = _experts(tile_expert, n_tiles, tile_slot, tile_next, xs, expert_w1[0], expert_w3[0], expert_w2[0], tm_e)
    out = _combine(pos1, pos2, x1, route, ys, norm_final_w[None, :], tm_d)
    return out.reshape(bsz, seq, d)
```

```python
import functools

import jax
import jax.numpy as jnp
from jax import lax
from jax.experimental import pallas as pl
from jax.experimental.pallas import tpu as pltpu

F32 = jnp.float32
BF16 = jnp.bfloat16
NORM_EPS = 1e-6
LRU_C = 8.0
LANES = 128
SUBLANES = 8
GDN_CHUNK = 64
HEAD_DIM = 128
CONV_W = 4
HALO = 2 * SUBLANES
CONV_ROWS = 128
VMEM_LIMIT = 56 * 1024 * 1024
HI = lax.Precision.HIGHEST


def _cparams(sem):
    return pltpu.CompilerParams(dimension_semantics=sem, vmem_limit_bytes=VMEM_LIMIT)


def _softplus(x):
    return jnp.maximum(x, 0.0) + jnp.log1p(jnp.exp(-jnp.abs(x)))


def _sigmoid(x):
    return 0.5 * (jnp.tanh(0.5 * x) + 1.0)


def _shift_matrix():
    r = lax.broadcasted_iota(jnp.int32, ((CONV_W - 1) * CONV_ROWS, CONV_ROWS + HALO), 0)
    c = lax.broadcasted_iota(jnp.int32, ((CONV_W - 1) * CONV_ROWS, CONV_ROWS + HALO), 1)
    return (c == (r % CONV_ROWS) + (r // CONV_ROWS) + HALO - (CONV_W - 1)).astype(BF16)


def _causal_conv(shift, xwin, w_ref, lanes):
    sh = _dot(shift, xwin)
    acc = w_ref[CONV_W - 1:CONV_W, lanes] * xwin[HALO:, :].astype(F32)
    for j in range(CONV_W - 1):
        acc = acc + w_ref[j:j + 1, lanes] * sh[j * CONV_ROWS:(j + 1) * CONV_ROWS]
    return acc


def _silu_half(hx):
    return hx + hx * jnp.tanh(hx)


def _silu(x):
    return _silu_half(0.5 * x)


def _dot(a, b, precision=None):
    return jnp.dot(a, b, preferred_element_type=F32, precision=precision)


def _dot_nt(a, b, precision=None):
    return lax.dot_general(a, b, (((1,), (1,)), ((), ())), preferred_element_type=F32,
                           precision=precision)


def _in_proj_kernel(x_ref, nw_ref, w_ref, wba_ref, alog_ref, dtb_ref, o_ref, gb_ref, gt_ref, h_scr,
                    *, n_heads):
    j = pl.program_id(1)

    @pl.when(j == 0)
    def _():
        x = x_ref[...]
        ms = jnp.mean(x * x, axis=-1, keepdims=True)
        h = (x * lax.rsqrt(ms + NORM_EPS) * nw_ref[...]).astype(BF16)
        h_scr[...] = h
        ba = _dot(h, wba_ref[...])
        lane = lax.broadcasted_iota(jnp.int32, ba.shape, 1)
        beta = _sigmoid(ba)
        g = -jnp.exp(alog_ref[...]) * _softplus(ba + dtb_ref[...])
        gb = jnp.where(lane < n_heads, beta, g)
        gb_ref[...] = gb
        gt_ref[...] = gb.T[:2 * n_heads, :]

    o_ref[0] = _dot(h_scr[...], w_ref[...]).astype(o_ref.dtype)


def _in_proj(x2, norm_w, w_main, w_ba, alog_l, dtb_l, tm, tn, n_heads):
    n, d = x2.shape
    npieces = w_main.shape[1] // tn
    return pl.pallas_call(
        functools.partial(_in_proj_kernel, n_heads=n_heads),
        grid=(n // tm, npieces),
        in_specs=[
            pl.BlockSpec((tm, d), lambda i, j: (i, 0)),
            pl.BlockSpec((1, d), lambda i, j: (0, 0)),
            pl.BlockSpec((d, tn), lambda i, j: (0, j)),
            pl.BlockSpec((d, LANES), lambda i, j: (0, 0)),
            pl.BlockSpec((1, LANES), lambda i, j: (0, 0)),
            pl.BlockSpec((1, LANES), lambda i, j: (0, 0)),
        ],
        out_specs=[
            pl.BlockSpec((1, tm, tn), lambda i, j: (j, i, 0)),
            pl.BlockSpec((tm, LANES), lambda i, j: (i, 0)),
            pl.BlockSpec((2 * n_heads, tm), lambda i, j: (0, i)),
        ],
        out_shape=[
            jax.ShapeDtypeStruct((npieces, n, tn), BF16),
            jax.ShapeDtypeStruct((n, LANES), F32),
            jax.ShapeDtypeStruct((2 * n_heads, n), F32),
        ],
        scratch_shapes=[pltpu.VMEM((tm, d), BF16)],
        compiler_params=_cparams(("arbitrary", "arbitrary")),
        name="in_proj",
    )(x2, norm_w, w_main, w_ba, alog_l, dtb_l)


def _gdn_kernel(q_ref, k_ref, v_ref, z_ref, gb_ref, gt_ref, cw_ref, nw_ref, o_ref,
                xbuf, qkv, state, *, n_heads, lc):
    t = pl.program_id(1)
    hd = HEAD_DIM
    width = n_heads * hd
    c = GDN_CHUNK

    @pl.when(t == 0)
    def _():
        xbuf[0:HALO, :] = jnp.zeros((HALO, 3 * width), BF16)
        state[...] = jnp.zeros_like(state)

    xbuf[HALO:HALO + lc, 0:width] = q_ref[0, 0]
    xbuf[HALO:HALO + lc, width:2 * width] = k_ref[0, 0]
    xbuf[HALO:HALO + lc, 2 * width:3 * width] = v_ref[0, 0]

    row = lax.broadcasted_iota(jnp.int32, (c, c), 0)
    col = lax.broadcasted_iota(jnp.int32, (c, c), 1)
    causal = row >= col
    strict = row > col
    eye = (row == col).astype(F32)
    ltri = causal.astype(F32)
    r2 = lax.broadcasted_iota(jnp.int32, (lc, lc), 0)
    c2 = lax.broadcasted_iota(jnp.int32, (lc, lc), 1)
    utri = ((r2 <= c2) & ((r2 // c) == (c2 // c))).astype(F32)
    grow_all = _dot(gt_ref[...], utri, HI)

    def l2n(x, scale=None):
        inv = lax.rsqrt(jnp.sum(x * x, axis=-1, keepdims=True) + NORM_EPS)
        return x * (inv if scale is None else inv * scale)

    nci = lc // c
    gcols = [_dot(ltri, gb_ref[ci * c:(ci + 1) * c, :], HI) for ci in range(nci)]

    shift = _shift_matrix()
    pair = 2 * hd
    for piece in range(3):
        for hp in range(width // pair):
            lanes = slice(piece * width + hp * pair, piece * width + (hp + 1) * pair)
            for sub in range(lc // CONV_ROWS):
                rows = slice(sub * CONV_ROWS, (sub + 1) * CONV_ROWS)
                y = _silu_half(_causal_conv(shift, xbuf[sub * CONV_ROWS:(sub + 1) * CONV_ROWS + HALO, lanes],
                                            cw_ref, lanes))
                for half in range(2):
                    yh = y[:, half * hd:(half + 1) * hd]
                    if piece == 0:
                        yh = l2n(yh, hd ** -0.5)
                    elif piece == 1:
                        yh = l2n(yh)
                    qkv[rows, lanes.start + half * hd:lanes.start + (half + 1) * hd] = yh
    xbuf[0:HALO, :] = xbuf[lc:lc + HALO, :]

    blk = SUBLANES * 2
    diag_blk = (row // blk) == (col // blk)
    bf = lambda v: v.astype(BF16)
    heads = range(n_heads)

    def prepare(cis):
        units = [(ci, h) for ci in cis for h in heads]
        eg_cols = {ci: jnp.exp(gcols[ci]) for ci in cis}
        st = {}
        for (ci, h) in units:
            sl = slice(ci * c, (ci + 1) * c)
            qn = qkv[sl, h * hd:(h + 1) * hd]
            kn = qkv[sl, width + h * hd:width + (h + 1) * hd]
            vv = qkv[sl, 2 * width + h * hd:2 * width + (h + 1) * hd]
            beta_b = jnp.broadcast_to(gb_ref[sl, h:h + 1], (c, hd))
            gc_b = jnp.broadcast_to(gcols[ci][:, n_heads + h:n_heads + h + 1], (c, hd))
            eg = jnp.broadcast_to(eg_cols[ci][:, n_heads + h:n_heads + h + 1], (c, hd))
            gc_row = grow_all[n_heads + h:n_heads + h + 1, sl]
            qkk = _dot_nt(bf(jnp.concatenate([qn, kn], axis=0)), bf(kn))
            dmat = gc_b[:, :c] - gc_row
            decay = jnp.where(causal, jnp.exp(jnp.minimum(dmat, 0.0)), 0.0)
            a = jnp.where(strict, beta_b[:, :c] * qkk[c:] * decay, 0.0)
            dm = jnp.where(diag_blk, a, 0.0)
            st[(ci, h)] = dict(
                attn=bf(qkk[:c] * decay), dm=dm, lm=bf(a - dm),
                rhs=bf(jnp.concatenate([vv * beta_b, kn * (beta_b * eg)], axis=1)),
                qg=qn * eg,
                kdt=bf(kn.T * jnp.exp(gc_row[:, c - 1:c] - gc_row)),
                sdec=eg[c - 1:c, :])

        for u in units:
            d = st[u]
            d["x"] = _dot(bf(d["dm"]), bf(d["dm"]))
            d["p"] = eye - d["dm"]
        nsq = blk.bit_length() - 2
        for it in range(nsq):
            for u in units:
                d = st[u]
                if it < nsq - 1:
                    px = _dot(bf(jnp.concatenate([d["p"], d["x"]], axis=0)), bf(d["x"]))
                    d["p"] = d["p"] + px[:c]
                    d["x"] = px[c:]
                else:
                    d["p"] = d["p"] + _dot(bf(d["p"]), bf(d["x"]))
        for u in units:
            d = st[u]
            pb = bf(d["p"])
            d["m"] = _dot(pb, d["lm"])
            d["y"] = _dot(pb, d["rhs"])
        npow = c // blk - 1
        for u in units:
            d = st[u]
            d["tp"] = eye - d["m"]
            d["mk"] = d["m"]
        for k in range(2, npow + 1):
            for u in units:
                d = st[u]
                d["mk"] = _dot(bf(d["m"]), bf(d["mk"]))
                d["tp"] = d["tp"] + d["mk"] if k % 2 == 0 else d["tp"] - d["mk"]
        for u in units:
            d = st[u]
            d["sol"] = _dot(bf(d["tp"]), bf(d["y"]))
        return st

    states = [state[h] for h in heads]
    st = prepare(range(nci))
    for ci in range(nci):
        sl = slice(ci * c, (ci + 1) * c)
        ws = {}
        for h in heads:
            d = st[(ci, h)]
            wq = bf(jnp.concatenate([d["sol"][:, hd:], d["qg"]], axis=0))
            ws[h] = _dot(wq, bf(states[h]))
        upd = {}
        vnew = {}
        for h in heads:
            d = st[(ci, h)]
            vnew[h] = bf(d["sol"][:, :hd] - ws[h][:c])
            upd[h] = _dot(jnp.concatenate([d["attn"], d["kdt"]], axis=0), vnew[h])
        for h in heads:
            d = st[(ci, h)]
            o = ws[h][c:] + upd[h][:c]
            states[h] = states[h] * d["sdec"] + upd[h][c:]
            on = o * lax.rsqrt(jnp.mean(o * o, axis=-1, keepdims=True) + NORM_EPS) * nw_ref[...]
            zz = z_ref[0, 0, sl, h * hd:(h + 1) * hd].astype(F32)
            o_ref[0, sl, h * hd:(h + 1) * hd] = (on * _silu(zz)).astype(o_ref.dtype)
    for h in heads:
        state[h] = states[h]


def _gdn(proj4, gb, gt, conv_w, norm_w, bsz, seq, n_heads, lc):
    width = n_heads * HEAD_DIM
    nt = seq // lc
    kern = functools.partial(_gdn_kernel, n_heads=n_heads, lc=lc)

    def piece(p):
        return pl.BlockSpec((1, 1, lc, width), lambda b, t, p=p: (p, b, t, 0))

    return pl.pallas_call(
        kern,
        grid=(bsz, nt),
        in_specs=[
            piece(0), piece(1), piece(2), piece(3),
            pl.BlockSpec((lc, LANES), lambda b, t: (b * nt + t, 0)),
            pl.BlockSpec((2 * n_heads, lc), lambda b, t: (0, b * nt + t)),
            pl.BlockSpec((CONV_W, 3 * width), lambda b, t: (0, 0)),
            pl.BlockSpec((1, HEAD_DIM), lambda b, t: (0, 0)),
        ],
        out_specs=pl.BlockSpec((1, lc, width), lambda b, t: (b, t, 0)),
        out_shape=jax.ShapeDtypeStruct((bsz, seq, width), BF16),
        scratch_shapes=[
            pltpu.VMEM((lc + HALO, 3 * width), BF16),
            pltpu.VMEM((lc, 3 * width), F32),
            pltpu.VMEM((n_heads, HEAD_DIM, HEAD_DIM), F32),
        ],
        compiler_params=_cparams(("arbitrary", "arbitrary")),
        name="gdn",
    )(proj4, proj4, proj4, proj4, gb, gt, conv_w, norm_w)


def _rglru_kernel(x_ref, g_ref, cw_ref, cb_ref, wri_ref, br_ref, bi_ref, lam_ref, nw_ref, o_ref,
                  xbuf, a_scr, b_scr, hlast, *, n_blocks, lr):
    t = pl.program_id(1)
    bd = LANES
    width = n_blocks * bd

    @pl.when(t == 0)
    def _():
        xbuf[0:HALO, :] = jnp.zeros((HALO, width), BF16)
        hlast[...] = jnp.zeros_like(hlast)

    xbuf[HALO:HALO + lr, :] = x_ref[0, 0]
    shift = _shift_matrix()
    pair = 2 * bd
    xc_pairs = []
    for p in range(width // pair):
        lanes = slice(p * pair, (p + 1) * pair)
        xc_pairs.append(jnp.concatenate(
            [_causal_conv(shift, xbuf[s0:s0 + CONV_ROWS + HALO, lanes], cw_ref, lanes)
             for s0 in range(0, lr, CONV_ROWS)], axis=0) + cb_ref[:, lanes])
    xbuf[0:HALO, :] = xbuf[lr:lr + HALO, :]

    sp_lam = _softplus(-lam_ref[...])
    sub = lax.broadcasted_iota(jnp.int32, (lr, bd), 0) % SUBLANES
    for n in range(n_blocks):
        sl = slice(n * bd, (n + 1) * bd)
        xb = xc_pairs[n // 2][:, (n % 2) * bd:(n % 2 + 1) * bd]
        gates = _dot(xb.astype(BF16), wri_ref[n])
        r = _sigmoid(gates[:, :bd] + br_ref[:, sl])
        i = _sigmoid(gates[:, bd:] + bi_ref[:, sl])
        log_a = -LRU_C * r * sp_lam[:, sl]
        a = jnp.exp(log_a)
        th = jnp.tanh(log_a)
        b = jnp.sqrt(-2.0 * th / (1.0 - th)) * (i * xb)
        for s in (1, 2, 4):
            a_s = pltpu.roll(a, s, axis=0)
            b_s = pltpu.roll(b, s, axis=0)
            m = sub >= s
            b = jnp.where(m, a * b_s + b, b)
            a = jnp.where(m, a * a_s, a)
        a_scr[:, sl] = a
        b_scr[:, sl] = b

    def tile_step(i, hprev):
        r0 = pl.multiple_of(i * SUBLANES, SUBLANES)
        hh = a_scr[pl.ds(r0, SUBLANES), :] * hprev + b_scr[pl.ds(r0, SUBLANES), :]
        b_scr[pl.ds(r0, SUBLANES), :] = hh
        return jnp.broadcast_to(hh[SUBLANES - 1:SUBLANES, :], (SUBLANES, width))

    hfin = lax.fori_loop(0, lr // SUBLANES, tile_step, hlast[...])
    hlast[...] = hfin

    for n in range(n_blocks):
        sl = slice(n * bd, (n + 1) * bd)
        gt = g_ref[0, 0, :, sl].astype(F32)
        y = b_scr[:, sl] * jax.nn.gelu(gt, approximate=True)
        yn = y * lax.rsqrt(jnp.mean(y * y, axis=-1, keepdims=True) + NORM_EPS) * nw_ref[:, sl]
        o_ref[0, :, sl] = yn.astype(o_ref.dtype)


def _rglru(proj4, conv_w, conv_b, wri, br, bi, lam, norm_w, bsz, seq, n_blocks, lr, px, pg):
    width = n_blocks * LANES
    nt = seq // lr
    kern = functools.partial(_rglru_kernel, n_blocks=n_blocks, lr=lr)

    def vec():
        return pl.BlockSpec((1, width), lambda b, t: (0, 0))

    return pl.pallas_call(
        kern,
        grid=(bsz, nt),
        in_specs=[
            pl.BlockSpec((1, 1, lr, width), lambda b, t: (px, b, t, 0)),
            pl.BlockSpec((1, 1, lr, width), lambda b, t: (pg, b, t, 0)),
            pl.BlockSpec((CONV_W, width), lambda b, t: (0, 0)),
            vec(),
            pl.BlockSpec((n_blocks, LANES, 2 * LANES), lambda b, t: (0, 0, 0)),
            vec(), vec(), vec(), vec(),
        ],
        out_specs=pl.BlockSpec((1, lr, width), lambda b, t: (b, t, 0)),
        out_shape=jax.ShapeDtypeStruct((bsz, seq, width), BF16),
        scratch_shapes=[
            pltpu.VMEM((lr + HALO, width), BF16),
            pltpu.VMEM((lr, width), F32),
            pltpu.VMEM((lr, width), F32),
            pltpu.VMEM((SUBLANES, width), F32),
        ],
        compiler_params=_cparams(("arbitrary", "arbitrary")),
        name="rglru",
    )(proj4, proj4, conv_w, conv_b, wri, br, bi, lam, norm_w)


def _out_route_kernel(yg_ref, yr_ref, x_ref, wog_ref, wor_ref, nw_ref, wr_ref, rb_ref,
                      x1_ref, h2_ref, route_ref, routet_ref, cnt_ref, cnt_scr, *, n_groups, per_group):
    i = pl.program_id(0)
    ts = x_ref.shape[0]

    @pl.when(i == 0)
    def _():
        cnt_scr[...] = jnp.zeros_like(cnt_scr)

    lane = lax.broadcasted_iota(jnp.int32, (ts, LANES), 1)
    neg = jnp.float32(-jnp.inf)
    big = jnp.int32(1 << 20)
    r = lax.broadcasted_iota(jnp.int32, (ts, ts), 0)
    c = lax.broadcasted_iota(jnp.int32, (ts, ts), 1)
    before = (r > c).astype(BF16)

    def first_argmax(vals, mask):
        mx = jnp.max(jnp.where(mask, vals, neg), axis=-1, keepdims=True)
        idx = jnp.min(jnp.where(mask & (vals == mx), lane, big), axis=-1, keepdims=True)
        return mx, idx

    x1 = x_ref[...] + _dot(yg_ref[...], wog_ref[...]) + _dot(yr_ref[...], wor_ref[...])
    x1_ref[...] = x1
    h2 = x1 * lax.rsqrt(jnp.mean(x1 * x1, axis=-1, keepdims=True) + NORM_EPS) * nw_ref[...]
    h2_ref[...] = h2
    h_hi = h2.astype(BF16)
    h_lo = (h2 - h_hi.astype(F32)).astype(BF16)
    hw = _dot(h_hi, wr_ref[...])
    logits = hw[:, :LANES] + hw[:, LANES:] + _dot(h_lo, wr_ref[:, :LANES]) + rb_ref[...]
    counts = cnt_scr[...]
    gmask = lane < n_groups
    gmax, gidx = first_argmax(logits, gmask)
    zg = jnp.sum(jnp.where(gmask, jnp.exp(logits - gmax), 0.0), axis=-1, keepdims=True)
    grp_p = 1.0 / zg
    lo = n_groups + gidx * per_group
    emask = (lane >= lo) & (lane < lo + per_group)
    m1, i1 = first_argmax(logits, emask)
    m2, i2 = first_argmax(logits, emask & (lane != i1))
    ze = jnp.sum(jnp.where(emask, jnp.exp(logits - m1), 0.0), axis=-1, keepdims=True)
    p1 = 1.0 / ze
    p2 = jnp.exp(m2 - m1) / ze
    den = p1 + p2
    g1 = grp_p * (p1 / den)
    g2 = grp_p * (p2 / den)
    e1 = i1 - n_groups
    e2 = i2 - n_groups
    oh1 = lane == e1
    oh2 = lane == e2
    onehot = jnp.where(oh1 | oh2, 1.0, 0.0)
    prefix = _dot(before, onehot.astype(BF16)) + counts
    rank1 = jnp.sum(jnp.where(oh1, prefix, 0.0), axis=-1, keepdims=True)
    rank2 = jnp.sum(jnp.where(oh2, prefix, 0.0), axis=-1, keepdims=True)
    counts = counts + jnp.sum(onehot, axis=0, keepdims=True)
    out = jnp.where(lane == 0, e1.astype(F32), 0.0)
    out = jnp.where(lane == 1, e2.astype(F32), out)
    out = jnp.where(lane == 2, rank1, out)
    out = jnp.where(lane == 3, rank2, out)
    out = jnp.where(lane == 4, g1, out)
    out = jnp.where(lane == 5, g2, out)
    route_ref[...] = out
    routet_ref[...] = out.T[:SUBLANES, :]
    cnt_scr[...] = counts
    cnt_ref[...] = counts


def _out_route(y_gdn, y_rg, x2, wo_g, wo_r, norm_w, w_router, b_router, tm, n_groups, per_group):
    n, d = x2.shape
    wg = y_gdn.shape[1]
    wr = y_rg.shape[1]
    kern = functools.partial(_out_route_kernel, n_groups=n_groups, per_group=per_group)
    return pl.pallas_call(
        kern,
        grid=(n // tm,),
        in_specs=[
            pl.BlockSpec((tm, wg), lambda i: (i, 0)),
            pl.BlockSpec((tm, wr), lambda i: (i, 0)),
            pl.BlockSpec((tm, d), lambda i: (i, 0)),
            pl.BlockSpec((wg, d), lambda i: (0, 0), pipeline_mode=pl.Buffered(1)),
            pl.BlockSpec((wr, d), lambda i: (0, 0), pipeline_mode=pl.Buffered(1)),
            pl.BlockSpec((1, d), lambda i: (0, 0)),
            pl.BlockSpec((d, 2 * LANES), lambda i: (0, 0), pipeline_mode=pl.Buffered(1)),
            pl.BlockSpec((1, LANES), lambda i: (0, 0)),
        ],
        out_specs=[
            pl.BlockSpec((tm, d), lambda i: (i, 0)),
            pl.BlockSpec((tm, d), lambda i: (i, 0)),
            pl.BlockSpec((tm, LANES), lambda i: (i, 0)),
            pl.BlockSpec((SUBLANES, tm), lambda i: (0, i)),
            pl.BlockSpec((1, LANES), lambda i: (0, 0)),
        ],
        out_shape=[
            jax.ShapeDtypeStruct((n, d), F32),
            jax.ShapeDtypeStruct((n, d), F32),
            jax.ShapeDtypeStruct((n, LANES), F32),
            jax.ShapeDtypeStruct((SUBLANES, n), F32),
            jax.ShapeDtypeStruct((1, LANES), F32),
        ],
        scratch_shapes=[pltpu.VMEM((1, LANES), F32)],
        compiler_params=_cparams(("arbitrary",)),
        name="out_route",
    )(y_gdn, y_rg, x2, wo_g, wo_r, norm_w, w_router, b_router)


def _dispatch_kernel(pos1_ref, pos2_ref, pad_ref, h_ref, xs_ref, zeros, sem, zsem, *, n_pad):
    i = pl.program_id(0)
    tm = h_ref.shape[0]
    tz = zeros.shape[0]

    @pl.when(i == 0)
    def _():
        zeros[...] = jnp.zeros_like(zeros)

        def zero_copy(e):
            row = pl.multiple_of(jnp.maximum(pad_ref[e], 0), SUBLANES)
            return pltpu.make_async_copy(zeros, xs_ref.at[pl.ds(row, tz), :], zsem)

        for e in range(n_pad):
            @pl.when(pad_ref[e] >= 0)
            def _():
                zero_copy(e).start()
        for e in range(n_pad):
            @pl.when(pad_ref[e] >= 0)
            def _():
                zero_copy(e).wait()

    def row_copy(r, pos_ref):
        return pltpu.make_async_copy(h_ref.at[pl.ds(r, 1), :],
                                     xs_ref.at[pl.ds(pos_ref[i * tm + r], 1), :], sem)

    def issue(r, carry):
        row_copy(r, pos1_ref).start(priority=0)
        row_copy(r, pos2_ref).start(priority=1)
        return carry

    lax.fori_loop(0, tm, issue, 0, unroll=8)
    for _ in range(2):
        pltpu.make_async_copy(h_ref, xs_ref.at[pl.ds(0, tm), :], sem).wait()


def _dispatch(pos1, pos2, pad_row, h2, n_rows, tm, tm_e):
    n, d = h2.shape
    return pl.pallas_call(
        functools.partial(_dispatch_kernel, n_pad=pad_row.shape[0]),
        grid_spec=pltpu.PrefetchScalarGridSpec(
            num_scalar_prefetch=3,
            grid=(n // tm,),
            in_specs=[pl.BlockSpec((tm, d), lambda i, p1, p2, pr: (i, 0))],
            out_specs=pl.BlockSpec(memory_space=pl.ANY),
            scratch_shapes=[pltpu.VMEM((tm_e, d), h2.dtype),
                            pltpu.SemaphoreType.DMA(()), pltpu.SemaphoreType.DMA(())],
        ),
        out_shape=jax.ShapeDtypeStruct((n_rows, d), h2.dtype),
        compiler_params=pltpu.CompilerParams(dimension_semantics=("arbitrary",),
                                             vmem_limit_bytes=VMEM_LIMIT, has_side_effects=True),
        name="dispatch",
    )(pos1, pos2, pad_row, h2)


def _experts_kernel(te_ref, nt_ref, slot_ref, next_ref, xs_ref, w1_ref, w3_ref, w2_ref, ys_ref,
                    w1f, w3f, w2f, w1b, w3b, w2b, sem):
    j = pl.program_id(0)
    prev = te_ref[jnp.maximum(j - 1, 0)]
    active = j < nt_ref[0]

    def weight_copies(e, buf):
        return [pltpu.make_async_copy(src.at[e], dst.at[buf], sem.at[buf])
                for src, dst in ((w1_ref, w1f), (w3_ref, w3f), (w2_ref, w2f))]

    @pl.when(active & ((j == 0) | (te_ref[j] != prev)))
    def _():
        buf = slot_ref[j]

        @pl.when(j == 0)
        def _():
            for cp in weight_copies(te_ref[0], 0):
                cp.start()

        for cp in weight_copies(te_ref[j], buf):
            cp.wait()

        @pl.when(next_ref[j] >= 0)
        def _():
            for cp in weight_copies(next_ref[j], 1 - buf):
                cp.start()

        w1b[...] = w1f[buf].astype(BF16)
        w3b[...] = w3f[buf].astype(BF16)
        w2b[...] = w2f[buf].astype(BF16)

    @pl.when(active)
    def _():
        x = xs_ref[...].astype(BF16)
        h1 = _dot(x, w1b[...])
        h3 = _dot(x, w3b[...])
        he = (_silu(h1) * h3).astype(BF16)
        ys_ref[...] = _dot(he, w2b[...])

    @pl.when(jnp.logical_not(active))
    def _():
        ys_ref[...] = jnp.zeros_like(ys_ref)


def _experts(tile_expert, n_tiles, tile_slot, tile_next, xs, w1, w3, w2, tm):
    n_rows, dp = xs.shape
    d, ff = w1.shape[1], w1.shape[2]

    def rowmap(j, te, nt, sl, nx):
        return (jnp.minimum(j, nt[0] - 1), 0)

    return pl.pallas_call(
        _experts_kernel,
        grid_spec=pltpu.PrefetchScalarGridSpec(
            num_scalar_prefetch=4,
            grid=(n_rows // tm,),
            in_specs=[
                pl.BlockSpec((tm, dp), rowmap),
                pl.BlockSpec(memory_space=pl.ANY),
                pl.BlockSpec(memory_space=pl.ANY),
                pl.BlockSpec(memory_space=pl.ANY),
            ],
            out_specs=pl.BlockSpec((tm, dp), lambda j, te, nt, sl, nx: (j, 0)),
            scratch_shapes=[
                pltpu.VMEM((2, d, ff), F32),
                pltpu.VMEM((2, d, ff), F32),
                pltpu.VMEM((2, ff, d), F32),
                pltpu.VMEM((d, ff), BF16),
                pltpu.VMEM((d, ff), BF16),
                pltpu.VMEM((ff, d), BF16),
                pltpu.SemaphoreType.DMA((2,)),
            ],
        ),
        out_shape=jax.ShapeDtypeStruct((n_rows, dp), F32),
        compiler_params=_cparams(("arbitrary",)),
        name="experts",
    )(tile_expert, n_tiles, tile_slot, tile_next, xs, w1, w3, w2)


def _combine_kernel(pos1_ref, pos2_ref, x1_ref, route_ref, ys_ref, nw_ref, o_ref, y1, y2, sem):
    i = pl.program_id(0)
    tm = x1_ref.shape[0]
    slot = i % 2

    def start_tile(tile, buf):
        def issue(r, carry):
            for prio, (pos_ref, dst) in enumerate(((pos1_ref, y1), (pos2_ref, y2))):
                pltpu.make_async_copy(ys_ref.at[pl.ds(pos_ref[tile * tm + r], 1), :],
                                      dst.at[buf, pl.ds(r, 1), :], sem.at[buf]).start(priority=prio)
            return carry

        lax.fori_loop(0, tm, issue, 0, unroll=8)

    @pl.when(i == 0)
    def _():
        start_tile(0, 0)

    @pl.when(i + 1 < pl.num_programs(0))
    def _():
        start_tile(i + 1, 1 - slot)

    for dst in (y1, y2):
        pltpu.make_async_copy(ys_ref.at[pl.ds(0, tm), :], dst.at[slot], sem.at[slot]).wait()

    route = route_ref[...]
    g1 = route[:, 4:5]
    g2 = route[:, 5:6]
    x = x1_ref[...] + (g1 * y1[slot] + g2 * y2[slot])
    o_ref[...] = x * lax.rsqrt(jnp.mean(x * x, axis=-1, keepdims=True) + NORM_EPS) * nw_ref[...]


def _combine(pos1, pos2, x1, route, ys, norm_w, tm):
    n, d = x1.shape
    return pl.pallas_call(
        _combine_kernel,
        grid_spec=pltpu.PrefetchScalarGridSpec(
            num_scalar_prefetch=2,
            grid=(n // tm,),
            in_specs=[
                pl.BlockSpec((tm, d), lambda i, p1, p2: (i, 0)),
                pl.BlockSpec((tm, LANES), lambda i, p1, p2: (i, 0)),
                pl.BlockSpec(memory_space=pl.ANY),
                pl.BlockSpec((1, d), lambda i, p1, p2: (0, 0)),
            ],
            out_specs=pl.BlockSpec((tm, d), lambda i, p1, p2: (i, 0)),
            scratch_shapes=[
                pltpu.VMEM((2, tm, ys.shape[1]), ys.dtype),
                pltpu.VMEM((2, tm, ys.shape[1]), ys.dtype),
                pltpu.SemaphoreType.DMA((2,)),
            ],
        ),
        out_shape=jax.ShapeDtypeStruct((n, d), F32),
        compiler_params=_cparams(("arbitrary",)),
        name="combine",
    )(pos1, pos2, x1, route, ys, norm_w)


def _pick(n, pref):
    t = min(n, pref)
    assert n % t == 0, (n, pref)
    return t


def _lookup(table, idx):
    k = jnp.arange(table.shape[0], dtype=jnp.int32)
    return jnp.sum(jnp.where(idx[:, None] == k[None, :], table[None, :], 0), axis=1)


def _lane_vec(v, offset=0):
    return jnp.zeros((1, LANES), F32).at[0, offset:offset + v.shape[0]].set(v.astype(F32))


def kernel(x, norm_mix_w, w_in, gdn_conv_w, gdn_a_log, gdn_dt_bias, gdn_norm_w, rg_conv_w, rg_conv_b, rg_w_rgate, rg_b_rgate, rg_w_igate, rg_b_igate, rg_lambda, rg_norm_w, w_out, norm_ffn_w, router_w_group, router_b_group, router_w_expert, router_b_expert, expert_w1, expert_w3, expert_w2, norm_final_w):
    bsz, seq, d = x.shape
    n = bsz * seq
    depth = w_in.shape[0]
    assert depth == 1
    n_heads = gdn_a_log.shape[1]
    gw = n_heads * HEAD_DIM
    rw = rg_lambda.shape[1]
    n_blocks = rg_w_rgate.shape[1]
    n_groups = router_w_group.shape[2]
    n_experts = expert_w1.shape[1]
    per_group = n_experts // n_groups
    assert gw == rw and rw == n_blocks * LANES
    assert n_groups + n_experts <= LANES and 2 * n_heads <= LANES

    x2 = x.reshape(n, d)
    wi = w_in[0]
    nb = 4 * gw
    w_main = jnp.concatenate([wi[:, :nb], wi[:, nb + 2 * n_heads:]], axis=1).astype(BF16)
    w_ba = jnp.pad(wi[:, nb:nb + 2 * n_heads], ((0, 0), (0, LANES - 2 * n_heads))).astype(BF16)
    alog_l = _lane_vec(gdn_a_log[0], n_heads)
    dtb_l = _lane_vec(gdn_dt_bias[0], n_heads)

    tm_in = _pick(n, 1024)
    proj, gb, gt = _in_proj(x2, norm_mix_w[0][None, :], w_main, w_ba, alog_l, dtb_l, tm_in, gw, n_heads)
    proj4 = proj.reshape(proj.shape[0], bsz, seq, gw)

    lc = _pick(seq, 4 * GDN_CHUNK)
    y_gdn = _gdn(proj4, gb, gt, 0.5 * gdn_conv_w[0], gdn_norm_w[0][None, :], bsz, seq, n_heads, lc)

    wri = jnp.concatenate([rg_w_rgate[0], rg_w_igate[0]], axis=-1).astype(BF16)
    lr = _pick(seq, 256)
    y_rg = _rglru(proj4, rg_conv_w[0], rg_conv_b[0][None, :], wri, rg_b_rgate[0][None, :],
                  rg_b_igate[0][None, :], rg_lambda[0][None, :], rg_norm_w[0][None, :],
                  bsz, seq, n_blocks, lr, 4, 5)

    wo = w_out[0].astype(BF16)
    w_router = jnp.pad(jnp.concatenate([router_w_group[0], router_w_expert[0]], axis=1),
                       ((0, 0), (0, LANES - n_groups - n_experts)))
    w_router_hi = w_router.astype(BF16)
    w_router_lo = (w_router - w_router_hi.astype(F32)).astype(BF16)
    w_router = jnp.concatenate([w_router_hi, w_router_lo], axis=1)
    b_router = _lane_vec(jnp.concatenate([router_b_group[0], router_b_expert[0]]))
    tm_o = _pick(n, 512)
    x1, h2, route, route_t, counts = _out_route(y_gdn.reshape(n, gw), y_rg.reshape(n, rw), x2, wo[:gw], wo[gw:],
                                       norm_ffn_w[0][None, :], w_router, b_router, tm_o,
                                       n_groups, per_group)

    tm_e = 256
    cnt = counts[0, :n_experts].astype(jnp.int32)
    tiles_per = (cnt + tm_e - 1) // tm_e
    tile_end = jnp.cumsum(tiles_per)
    tile_start = tile_end - tiles_per
    max_tiles = (2 * n) // tm_e + n_experts
    n_rows = max_tiles * tm_e
    n_tiles = tile_end[-1:].astype(jnp.int32)
    jt = jnp.minimum(jnp.arange(max_tiles, dtype=jnp.int32), n_tiles[0] - 1)
    tile_expert = jnp.sum((tile_end[None, :] <= jt[:, None]).astype(jnp.int32), axis=1)
    row_start = tile_start * tm_e
    has_tiles = tiles_per > 0
    eidx = jnp.arange(n_experts, dtype=jnp.int32)
    later = jnp.where(has_tiles[None, :] & (eidx[None, :] > eidx[:, None]), eidx[None, :], n_experts)
    next_expert = jnp.min(later, axis=1)
    next_expert = jnp.where(next_expert < n_experts, next_expert, -1).astype(jnp.int32)
    expert_slot = ((jnp.cumsum(has_tiles.astype(jnp.int32)) - 1) % 2).astype(jnp.int32)
    tile_slot = _lookup(expert_slot, tile_expert)
    tile_next = _lookup(next_expert, tile_expert)
    tail = n_tiles[0] + jnp.arange(n_experts, dtype=jnp.int32)
    pad_row = jnp.concatenate([jnp.where(tiles_per > 0, (tile_end - 1) * tm_e, -1),
                               jnp.where(tail < max_tiles, tail * tm_e, -1)]).astype(jnp.int32)
    ri = route_t[:4].astype(jnp.int32)
    pos1 = _lookup(row_start, ri[0]) + ri[2]
    pos2 = _lookup(row_start, ri[1]) + ri[3]

    tm_d = _pick(n, 512)
    xs = _dispatch(pos1, pos2, pad_row, h2, n_rows, tm_d, tm_e)
    ys = _experts(tile_expert, n_tiles, tile_slot, tile_next, xs, expert_w1[0], expert_w3[0], expert_w2[0], tm_e)
    out = _combine(pos1, pos2, x1, route, ys, norm_final_w[None, :], tm_d)
    return out.reshape(bsz, seq, d)
```

```python
import functools

import jax
import jax.numpy as jnp
from jax import lax
from jax.experimental import pallas as pl
from jax.experimental.pallas import tpu as pltpu

F32 = jnp.float32
BF16 = jnp.bfloat16
NORM_EPS = 1e-6
LRU_C = 8.0
LANES = 128
SUBLANES = 8
GDN_CHUNK = 64
HEAD_DIM = 128
CONV_W = 4
HALO = 2 * SUBLANES
CONV_ROWS = 128
VMEM_LIMIT = 56 * 1024 * 1024
HI = lax.Precision.HIGHEST


def _cparams(sem):
    return pltpu.CompilerParams(dimension_semantics=sem, vmem_limit_bytes=VMEM_LIMIT)


def _softplus(x):
    return jnp.maximum(x, 0.0) + jnp.log1p(jnp.exp(-jnp.abs(x)))


def _sigmoid(x):
    return 0.5 * (jnp.tanh(0.5 * x) + 1.0)


def _shift_matrix():
    r = lax.broadcasted_iota(jnp.int32, ((CONV_W - 1) * CONV_ROWS, CONV_ROWS + HALO), 0)
    c = lax.broadcasted_iota(jnp.int32, ((CONV_W - 1) * CONV_ROWS, CONV_ROWS + HALO), 1)
    return (c == (r % CONV_ROWS) + (r // CONV_ROWS) + HALO - (CONV_W - 1)).astype(BF16)


def _causal_conv(shift, xwin, w_ref, lanes):
    sh = _dot(shift, xwin)
    acc = w_ref[CONV_W - 1:CONV_W, lanes] * xwin[HALO:, :].astype(F32)
    for j in range(CONV_W - 1):
        acc = acc + w_ref[j:j + 1, lanes] * sh[j * CONV_ROWS:(j + 1) * CONV_ROWS]
    return acc


def _silu_half(hx):
    return hx + hx * jnp.tanh(hx)


def _silu(x):
    return _silu_half(0.5 * x)


def _dot(a, b, precision=None):
    return jnp.dot(a, b, preferred_element_type=F32, precision=precision)


def _dot_nt(a, b, precision=None):
    return lax.dot_general(a, b, (((1,), (1,)), ((), ())), preferred_element_type=F32,
                           precision=precision)


def _in_proj_kernel(x_ref, nw_ref, w_ref, wba_ref, alog_ref, dtb_ref, o_ref, gb_ref, gt_ref, h_scr,
                    *, n_heads):
    j = pl.program_id(1)

    @pl.when(j == 0)
    def _():
        x = x_ref[...]
        ms = jnp.mean(x * x, axis=-1, keepdims=True)
        h = (x * lax.rsqrt(ms + NORM_EPS) * nw_ref[...]).astype(BF16)
        h_scr[...] = h
        ba = _dot(h, wba_ref[...])
        lane = lax.broadcasted_iota(jnp.int32, ba.shape, 1)
        beta = _sigmoid(ba)
        g = -jnp.exp(alog_ref[...]) * _softplus(ba + dtb_ref[...])
        gb = jnp.where(lane < n_heads, beta, g)
        gb_ref[...] = gb
        gt_ref[...] = gb.T[:2 * n_heads, :]

    o_ref[0] = _dot(h_scr[...], w_ref[...]).astype(o_ref.dtype)


def _in_proj(x2, norm_w, w_main, w_ba, alog_l, dtb_l, tm, tn, n_heads):
    n, d = x2.shape
    npieces = w_main.shape[1] // tn
    return pl.pallas_call(
        functools.partial(_in_proj_kernel, n_heads=n_heads),
        grid=(n // tm, npieces),
        in_specs=[
            pl.BlockSpec((tm, d), lambda i, j: (i, 0)),
            pl.BlockSpec((1, d), lambda i, j: (0, 0)),
            pl.BlockSpec((d, tn), lambda i, j: (0, j)),
            pl.BlockSpec((d, LANES), lambda i, j: (0, 0)),
            pl.BlockSpec((1, LANES), lambda i, j: (0, 0)),
            pl.BlockSpec((1, LANES), lambda i, j: (0, 0)),
        ],
        out_specs=[
            pl.BlockSpec((1, tm, tn), lambda i, j: (j, i, 0)),
            pl.BlockSpec((tm, LANES), lambda i, j: (i, 0)),
            pl.BlockSpec((2 * n_heads, tm), lambda i, j: (0, i)),
        ],
        out_shape=[
            jax.ShapeDtypeStruct((npieces, n, tn), BF16),
            jax.ShapeDtypeStruct((n, LANES), F32),
            jax.ShapeDtypeStruct((2 * n_heads, n), F32),
        ],
        scratch_shapes=[pltpu.VMEM((tm, d), BF16)],
        compiler_params=_cparams(("arbitrary", "arbitrary")),
        name="in_proj",
    )(x2, norm_w, w_main, w_ba, alog_l, dtb_l)


def _gdn_kernel(q_ref, k_ref, v_ref, z_ref, gb_ref, gt_ref, cw_ref, nw_ref, o_ref,
                xbuf, qkv, state, *, n_heads, lc):
    t = pl.program_id(1)
    hd = HEAD_DIM
    width = n_heads * hd
    c = GDN_CHUNK

    @pl.when(t == 0)
    def _():
        xbuf[0:HALO, :] = jnp.zeros((HALO, 3 * width), BF16)
        state[...] = jnp.zeros_like(state)

    xbuf[HALO:HALO + lc, 0:width] = q_ref[0, 0]
    xbuf[HALO:HALO + lc, width:2 * width] = k_ref[0, 0]
    xbuf[HALO:HALO + lc, 2 * width:3 * width] = v_ref[0, 0]

    row = lax.broadcasted_iota(jnp.int32, (c, c), 0)
    col = lax.broadcasted_iota(jnp.int32, (c, c), 1)
    causal = row >= col
    strict = row > col
    eye = (row == col).astype(F32)
    ltri = causal.astype(F32)
    r2 = lax.broadcasted_iota(jnp.int32, (lc, lc), 0)
    c2 = lax.broadcasted_iota(jnp.int32, (lc, lc), 1)
    utri = ((r2 <= c2) & ((r2 // c) == (c2 // c))).astype(F32)
    grow_all = _dot(gt_ref[...], utri, HI)

    def l2n(x, scale=None):
        inv = lax.rsqrt(jnp.sum(x * x, axis=-1, keepdims=True) + NORM_EPS)
        return x * (inv if scale is None else inv * scale)

    nci = lc // c
    gcols = [_dot(ltri, gb_ref[ci * c:(ci + 1) * c, :], HI) for ci in range(nci)]

    shift = _shift_matrix()
    pair = 2 * hd
    for piece in range(3):
        for hp in range(width // pair):
            lanes = slice(piece * width + hp * pair, piece * width + (hp + 1) * pair)
            for sub in range(lc // CONV_ROWS):
                rows = slice(sub * CONV_ROWS, (sub + 1) * CONV_ROWS)
                y = _silu_half(_causal_conv(shift, xbuf[sub * CONV_ROWS:(sub + 1) * CONV_ROWS + HALO, lanes],
                                            cw_ref, lanes))
                for half in range(2):
                    yh = y[:, half * hd:(half + 1) * hd]
                    if piece == 0:
                        yh = l2n(yh, hd ** -0.5)
                    elif piece == 1:
                        yh = l2n(yh)
                    qkv[rows, lanes.start + half * hd:lanes.start + (half + 1) * hd] = yh
    xbuf[0:HALO, :] = xbuf[lc:lc + HALO, :]

    blk = SUBLANES * 2
    diag_blk = (row // blk) == (col // blk)
    bf = lambda v: v.astype(BF16)
    heads = range(n_heads)

    def prepare(cis):
        units = [(ci, h) for ci in cis for h in heads]
        eg_cols = {ci: jnp.exp(gcols[ci]) for ci in cis}
        st = {}
        for (ci, h) in units:
            sl = slice(ci * c, (ci + 1) * c)
            qn = qkv[sl, h * hd:(h + 1) * hd]
            kn = qkv[sl, width + h * hd:width + (h + 1) * hd]
            vv = qkv[sl, 2 * width + h * hd:2 * width + (h + 1) * hd]
            beta_b = jnp.broadcast_to(gb_ref[sl, h:h + 1], (c, hd))
            gc_b = jnp.broadcast_to(gcols[ci][:, n_heads + h:n_heads + h + 1], (c, hd))
            eg = jnp.broadcast_to(eg_cols[ci][:, n_heads + h:n_heads + h + 1], (c, hd))
            gc_row = grow_all[n_heads + h:n_heads + h + 1, sl]
            qkk = _dot_nt(bf(jnp.concatenate([qn, kn], axis=0)), bf(kn))
            dmat = gc_b[:, :c] - gc_row
            decay = jnp.where(causal, jnp.exp(jnp.minimum(dmat, 0.0)), 0.0)
            a = jnp.where(strict, beta_b[:, :c] * qkk[c:] * decay, 0.0)
            dm = jnp.where(diag_blk, a, 0.0)
            st[(ci, h)] = dict(
                attn=bf(qkk[:c] * decay), dm=dm, lm=bf(a - dm),
                rhs=bf(jnp.concatenate([vv * beta_b, kn * (beta_b * eg)], axis=1)),
                qg=qn * eg,
                kdt=bf(kn.T * jnp.exp(gc_row[:, c - 1:c] - gc_row)),
                sdec=eg[c - 1:c, :])

        for u in units:
            d = st[u]
            d["x"] = _dot(bf(d["dm"]), bf(d["dm"]))
            d["p"] = eye - d["dm"]
        nsq = blk.bit_length() - 2
        for it in range(nsq):
            for u in units:
                d = st[u]
                if it < nsq - 1:
                    px = _dot(bf(jnp.concatenate([d["p"], d["x"]], axis=0)), bf(d["x"]))
                    d["p"] = d["p"] + px[:c]
                    d["x"] = px[c:]
                else:
                    d["p"] = d["p"] + _dot(bf(d["p"]), bf(d["x"]))
        for u in units:
            d = st[u]
            pb = bf(d["p"])
            d["m"] = _dot(pb, d["lm"])
            d["y"] = _dot(pb, d["rhs"])
        npow = c // blk - 1
        for u in units:
            d = st[u]
            d["tp"] = eye - d["m"]
            d["mk"] = d["m"]
        for k in range(2, npow + 1):
            for u in units:
                d = st[u]
                d["mk"] = _dot(bf(d["m"]), bf(d["mk"]))
                d["tp"] = d["tp"] + d["mk"] if k % 2 == 0 else d["tp"] - d["mk"]
        for u in units:
            d = st[u]
            d["sol"] = _dot(bf(d["tp"]), bf(d["y"]))
        return st

    states = [state[h] for h in heads]
    st = prepare(range(nci))
    for ci in range(nci):
        sl = slice(ci * c, (ci + 1) * c)
        ws = {}
        for h in heads:
            d = st[(ci, h)]
            wq = bf(jnp.concatenate([d["sol"][:, hd:], d["qg"]], axis=0))
            ws[h] = _dot(wq, bf(states[h]))
        upd = {}
        vnew = {}
        for h in heads:
            d = st[(ci, h)]
            vnew[h] = bf(d["sol"][:, :hd] - ws[h][:c])
            upd[h] = _dot(jnp.concatenate([d["attn"], d["kdt"]], axis=0), vnew[h])
        for h in heads:
            d = st[(ci, h)]
            o = ws[h][c:] + upd[h][:c]
            states[h] = states[h] * d["sdec"] + upd[h][c:]
            on = o * lax.rsqrt(jnp.mean(o * o, axis=-1, keepdims=True) + NORM_EPS) * nw_ref[...]
            zz = z_ref[0, 0, sl, h * hd:(h + 1) * hd].astype(F32)
            o_ref[0, sl, h * hd:(h + 1) * hd] = (on * _silu(zz)).astype(o_ref.dtype)
    for h in heads:
        state[h] = states[h]


def _gdn(proj4, gb, gt, conv_w, norm_w, bsz, seq, n_heads, lc):
    width = n_heads * HEAD_DIM
    nt = seq // lc
    kern = functools.partial(_gdn_kernel, n_heads=n_heads, lc=lc)

    def piece(p):
        return pl.BlockSpec((1, 1, lc, width), lambda b, t, p=p: (p, b, t, 0))

    return pl.pallas_call(
        kern,
        grid=(bsz, nt),
        in_specs=[
            piece(0), piece(1), piece(2), piece(3),
            pl.BlockSpec((lc, LANES), lambda b, t: (b * nt + t, 0)),
            pl.BlockSpec((2 * n_heads, lc), lambda b, t: (0, b * nt + t)),
            pl.BlockSpec((CONV_W, 3 * width), lambda b, t: (0, 0)),
            pl.BlockSpec((1, HEAD_DIM), lambda b, t: (0, 0)),
        ],
        out_specs=pl.BlockSpec((1, lc, width), lambda b, t: (b, t, 0)),
        out_shape=jax.ShapeDtypeStruct((bsz, seq, width), BF16),
        scratch_shapes=[
            pltpu.VMEM((lc + HALO, 3 * width), BF16),
            pltpu.VMEM((lc, 3 * width), F32),
            pltpu.VMEM((n_heads, HEAD_DIM, HEAD_DIM), F32),
        ],
        compiler_params=_cparams(("arbitrary", "arbitrary")),
        name="gdn",
    )(proj4, proj4, proj4, proj4, gb, gt, conv_w, norm_w)


def _rglru_kernel(x_ref, g_ref, cw_ref, cb_ref, wri_ref, br_ref, bi_ref, lam_ref, nw_ref, o_ref,
                  xbuf, a_scr, b_scr, hlast, *, n_blocks, lr):
    t = pl.program_id(1)
    bd = LANES
    width = n_blocks * bd

    @pl.when(t == 0)
    def _():
        xbuf[0:HALO, :] = jnp.zeros((HALO, width), BF16)
        hlast[...] = jnp.zeros_like(hlast)

    xbuf[HALO:HALO + lr, :] = x_ref[0, 0]
    shift = _shift_matrix()
    pair = 2 * bd
    xc_pairs = []
    for p in range(width // pair):
        lanes = slice(p * pair, (p + 1) * pair)
        xc_pairs.append(jnp.concatenate(
            [_causal_conv(shift, xbuf[s0:s0 + CONV_ROWS + HALO, lanes], cw_ref, lanes)
             for s0 in range(0, lr, CONV_ROWS)], axis=0) + cb_ref[:, lanes])
    xbuf[0:HALO, :] = xbuf[lr:lr + HALO, :]

    sp_lam = _softplus(-lam_ref[...])
    sub = lax.broadcasted_iota(jnp.int32, (lr, bd), 0) % SUBLANES
    for n in range(n_blocks):
        sl = slice(n * bd, (n + 1) * bd)
        xb = xc_pairs[n // 2][:, (n % 2) * bd:(n % 2 + 1) * bd]
        gates = _dot(xb.astype(BF16), wri_ref[n])
        r = _sigmoid(gates[:, :bd] + br_ref[:, sl])
        i = _sigmoid(gates[:, bd:] + bi_ref[:, sl])
        log_a = -LRU_C * r * sp_lam[:, sl]
        a = jnp.exp(log_a)
        th = jnp.tanh(log_a)
        b = jnp.sqrt(-2.0 * th / (1.0 - th)) * (i * xb)
        for s in (1, 2, 4):
            a_s = pltpu.roll(a, s, axis=0)
            b_s = pltpu.roll(b, s, axis=0)
            m = sub >= s
            b = jnp.where(m, a * b_s + b, b)
            a = jnp.where(m, a * a_s, a)
        a_scr[:, sl] = a
        b_scr[:, sl] = b

    def tile_step(i, hprev):
        r0 = pl.multiple_of(i * SUBLANES, SUBLANES)
        hh = a_scr[pl.ds(r0, SUBLANES), :] * hprev + b_scr[pl.ds(r0, SUBLANES), :]
        b_scr[pl.ds(r0, SUBLANES), :] = hh
        return jnp.broadcast_to(hh[SUBLANES - 1:SUBLANES, :], (SUBLANES, width))

    hfin = lax.fori_loop(0, lr // SUBLANES, tile_step, hlast[...])
    hlast[...] = hfin

    for n in range(n_blocks):
        sl = slice(n * bd, (n + 1) * bd)
        gt = g_ref[0, 0, :, sl].astype(F32)
        y = b_scr[:, sl] * jax.nn.gelu(gt, approximate=True)
        yn = y * lax.rsqrt(jnp.mean(y * y, axis=-1, keepdims=True) + NORM_EPS) * nw_ref[:, sl]
        o_ref[0, :, sl] = yn.astype(o_ref.dtype)


def _rglru(proj4, conv_w, conv_b, wri, br, bi, lam, norm_w, bsz, seq, n_blocks, lr, px, pg):
    width = n_blocks * LANES
    nt = seq // lr
    kern = functools.partial(_rglru_kernel, n_blocks=n_blocks, lr=lr)

    def vec():
        return pl.BlockSpec((1, width), lambda b, t: (0, 0))

    return pl.pallas_call(
        kern,
        grid=(bsz, nt),
        in_specs=[
            pl.BlockSpec((1, 1, lr, width), lambda b, t: (px, b, t, 0)),
            pl.BlockSpec((1, 1, lr, width), lambda b, t: (pg, b, t, 0)),
            pl.BlockSpec((CONV_W, width), lambda b, t: (0, 0)),
            vec(),
            pl.BlockSpec((n_blocks, LANES, 2 * LANES), lambda b, t: (0, 0, 0)),
            vec(), vec(), vec(), vec(),
        ],
        out_specs=pl.BlockSpec((1, lr, width), lambda b, t: (b, t, 0)),
        out_shape=jax.ShapeDtypeStruct((bsz, seq, width), BF16),
        scratch_shapes=[
            pltpu.VMEM((lr + HALO, width), BF16),
            pltpu.VMEM((lr, width), F32),
            pltpu.VMEM((lr, width), F32),
            pltpu.VMEM((SUBLANES, width), F32),
        ],
        compiler_params=_cparams(("arbitrary", "arbitrary")),
        name="rglru",
    )(proj4, proj4, conv_w, conv_b, wri, br, bi, lam, norm_w)


def _out_route_kernel(yg_ref, yr_ref, x_ref, wog_ref, wor_ref, nw_ref, wr_ref, rb_ref,
                      x1_ref, h2_ref, route_ref, routet_ref, cnt_ref, cnt_scr, *, n_groups, per_group):
    i = pl.program_id(0)
    ts = x_ref.shape[0]

    @pl.when(i == 0)
    def _():
        cnt_scr[...] = jnp.zeros_like(cnt_scr)

    lane = lax.broadcasted_iota(jnp.int32, (ts, LANES), 1)
    neg = jnp.float32(-jnp.inf)
    big = jnp.int32(1 << 20)
    r = lax.broadcasted_iota(jnp.int32, (ts, ts), 0)
    c = lax.broadcasted_iota(jnp.int32, (ts, ts), 1)
    before = (r > c).astype(BF16)

    def first_argmax(vals, mask):
        mx = jnp.max(jnp.where(mask, vals, neg), axis=-1, keepdims=True)
        idx = jnp.min(jnp.where(mask & (vals == mx), lane, big), axis=-1, keepdims=True)
        return mx, idx

    x1 = x_ref[...] + _dot(yg_ref[...], wog_ref[...]) + _dot(yr_ref[...], wor_ref[...])
    x1_ref[...] = x1
    h2 = x1 * lax.rsqrt(jnp.mean(x1 * x1, axis=-1, keepdims=True) + NORM_EPS) * nw_ref[...]
    h2_ref[...] = h2
    h_hi = h2.astype(BF16)
    h_lo = (h2 - h_hi.astype(F32)).astype(BF16)
    hw = _dot(h_hi, wr_ref[...])
    logits = hw[:, :LANES] + hw[:, LANES:] + _dot(h_lo, wr_ref[:, :LANES]) + rb_ref[...]
    counts = cnt_scr[...]
    gmask = lane < n_groups
    gmax, gidx = first_argmax(logits, gmask)
    zg = jnp.sum(jnp.where(gmask, jnp.exp(logits - gmax), 0.0), axis=-1, keepdims=True)
    grp_p = 1.0 / zg
    lo = n_groups + gidx * per_group
    emask = (lane >= lo) & (lane < lo + per_group)
    m1, i1 = first_argmax(logits, emask)
    m2, i2 = first_argmax(logits, emask & (lane != i1))
    ze = jnp.sum(jnp.where(emask, jnp.exp(logits - m1), 0.0), axis=-1, keepdims=True)
    p1 = 1.0 / ze
    p2 = jnp.exp(m2 - m1) / ze
    den = p1 + p2
    g1 = grp_p * (p1 / den)
    g2 = grp_p * (p2 / den)
    e1 = i1 - n_groups
    e2 = i2 - n_groups
    oh1 = lane == e1
    oh2 = lane == e2
    onehot = jnp.where(oh1 | oh2, 1.0, 0.0)
    prefix = _dot(before, onehot.astype(BF16)) + counts
    rank1 = jnp.sum(jnp.where(oh1, prefix, 0.0), axis=-1, keepdims=True)
    rank2 = jnp.sum(jnp.where(oh2, prefix, 0.0), axis=-1, keepdims=True)
    counts = counts + jnp.sum(onehot, axis=0, keepdims=True)
    out = jnp.where(lane == 0, e1.astype(F32), 0.0)
    out = jnp.where(lane == 1, e2.astype(F32), out)
    out = jnp.where(lane == 2, rank1, out)
    out = jnp.where(lane == 3, rank2, out)
    out = jnp.where(lane == 4, g1, out)
    out = jnp.where(lane == 5, g2, out)
    route_ref[...] = out
    routet_ref[...] = out.T[:SUBLANES, :]
    cnt_scr[...] = counts
    cnt_ref[...] = counts


def _out_route(y_gdn, y_rg, x2, wo, norm_w, w_router, b_router, tm, n_groups, per_group):
    n, d = x2.shape
    wg = y_gdn.shape[1]
    wr = y_rg.shape[1]
    assert wg == wr and wo.shape[0] == wg + wr
    kern = functools.partial(_out_route_kernel, n_groups=n_groups, per_group=per_group)
    return pl.pallas_call(
        kern,
        grid=(n // tm,),
        in_specs=[
            pl.BlockSpec((tm, wg), lambda i: (i, 0)),
            pl.BlockSpec((tm, wr), lambda i: (i, 0)),
            pl.BlockSpec((tm, d), lambda i: (i, 0)),
            pl.BlockSpec((wg, d), lambda i: (0, 0), pipeline_mode=pl.Buffered(1)),
            pl.BlockSpec((wr, d), lambda i: (1, 0), pipeline_mode=pl.Buffered(1)),
            pl.BlockSpec((1, d), lambda i: (0, 0)),
            pl.BlockSpec((d, 2 * LANES), lambda i: (0, 0), pipeline_mode=pl.Buffered(1)),
            pl.BlockSpec((1, LANES), lambda i: (0, 0)),
        ],
        out_specs=[
            pl.BlockSpec((tm, d), lambda i: (i, 0)),
            pl.BlockSpec((tm, d), lambda i: (i, 0)),
            pl.BlockSpec((tm, LANES), lambda i: (i, 0)),
            pl.BlockSpec((SUBLANES, tm), lambda i: (0, i)),
            pl.BlockSpec((1, LANES), lambda i: (0, 0)),
        ],
        out_shape=[
            jax.ShapeDtypeStruct((n, d), F32),
            jax.ShapeDtypeStruct((n, d), F32),
            jax.ShapeDtypeStruct((n, LANES), F32),
            jax.ShapeDtypeStruct((SUBLANES, n), F32),
            jax.ShapeDtypeStruct((1, LANES), F32),
        ],
        scratch_shapes=[pltpu.VMEM((1, LANES), F32)],
        compiler_params=_cparams(("arbitrary",)),
        name="out_route",
    )(y_gdn, y_rg, x2, wo, wo, norm_w, w_router, b_router)


def _dispatch_kernel(pos1_ref, pos2_ref, pad_ref, h_ref, xs_ref, zeros, sem, zsem, *, n_pad):
    i = pl.program_id(0)
    tm = h_ref.shape[0]
    tz = zeros.shape[0]

    @pl.when(i == 0)
    def _():
        zeros[...] = jnp.zeros_like(zeros)

        def zero_copy(e):
            row = pl.multiple_of(jnp.maximum(pad_ref[e], 0), SUBLANES)
            return pltpu.make_async_copy(zeros, xs_ref.at[pl.ds(row, tz), :], zsem)

        for e in range(n_pad):
            @pl.when(pad_ref[e] >= 0)
            def _():
                zero_copy(e).start()
        for e in range(n_pad):
            @pl.when(pad_ref[e] >= 0)
            def _():
                zero_copy(e).wait()

    def row_copy(r, pos_ref):
        return pltpu.make_async_copy(h_ref.at[pl.ds(r, 1), :],
                                     xs_ref.at[pl.ds(pos_ref[i * tm + r], 1), :], sem)

    for r in range(tm):
        row_copy(r, pos1_ref).start(priority=0)
        row_copy(r, pos2_ref).start(priority=1)
    for _ in range(2):
        pltpu.make_async_copy(h_ref, xs_ref.at[pl.ds(0, tm), :], sem).wait()


def _dispatch(pos1, pos2, pad_row, h2, n_rows, tm, tm_e):
    n, d = h2.shape
    return pl.pallas_call(
        functools.partial(_dispatch_kernel, n_pad=pad_row.shape[0]),
        grid_spec=pltpu.PrefetchScalarGridSpec(
            num_scalar_prefetch=3,
            grid=(n // tm,),
            in_specs=[pl.BlockSpec((tm, d), lambda i, p1, p2, pr: (i, 0))],
            out_specs=pl.BlockSpec(memory_space=pl.ANY),
            scratch_shapes=[pltpu.VMEM((tm_e, d), h2.dtype),
                            pltpu.SemaphoreType.DMA(()), pltpu.SemaphoreType.DMA(())],
        ),
        out_shape=jax.ShapeDtypeStruct((n_rows, d), h2.dtype),
        compiler_params=pltpu.CompilerParams(dimension_semantics=("arbitrary",),
                                             vmem_limit_bytes=VMEM_LIMIT, has_side_effects=True),
        name="dispatch",
    )(pos1, pos2, pad_row, h2)


def _experts_kernel(te_ref, nt_ref, slot_ref, next_ref, xs_ref, w1_ref, w3_ref, w2_ref, ys_ref,
                    w1f, w3f, w2f, w1b, w3b, w2b, sem):
    j = pl.program_id(0)
    prev = te_ref[jnp.maximum(j - 1, 0)]
    active = j < nt_ref[0]

    def weight_copies(e, buf):
        return [pltpu.make_async_copy(src.at[e], dst.at[buf], sem.at[buf])
                for src, dst in ((w1_ref, w1f), (w3_ref, w3f), (w2_ref, w2f))]

    @pl.when(active & ((j == 0) | (te_ref[j] != prev)))
    def _():
        buf = slot_ref[j]

        @pl.when(j == 0)
        def _():
            for cp in weight_copies(te_ref[0], 0):
                cp.start()

        for cp in weight_copies(te_ref[j], buf):
            cp.wait()

        @pl.when(next_ref[j] >= 0)
        def _():
            for cp in weight_copies(next_ref[j], 1 - buf):
                cp.start()

        w1b[...] = w1f[buf].astype(BF16)
        w3b[...] = w3f[buf].astype(BF16)
        w2b[...] = w2f[buf].astype(BF16)

    @pl.when(active)
    def _():
        x = xs_ref[...].astype(BF16)
        h1 = _dot(x, w1b[...])
        h3 = _dot(x, w3b[...])
        he = (_silu(h1) * h3).astype(BF16)
        ys_ref[...] = _dot(he, w2b[...])

    @pl.when(jnp.logical_not(active))
    def _():
        ys_ref[...] = jnp.zeros_like(ys_ref)


def _experts(tile_expert, n_tiles, tile_slot, tile_next, xs, w1, w3, w2, tm):
    n_rows, dp = xs.shape
    d, ff = w1.shape[1], w1.shape[2]

    def rowmap(j, te, nt, sl, nx):
        return (jnp.minimum(j, nt[0] - 1), 0)

    return pl.pallas_call(
        _experts_kernel,
        grid_spec=pltpu.PrefetchScalarGridSpec(
            num_scalar_prefetch=4,
            grid=(n_rows // tm,),
            in_specs=[
                pl.BlockSpec((tm, dp), rowmap),
                pl.BlockSpec(memory_space=pl.ANY),
                pl.BlockSpec(memory_space=pl.ANY),
                pl.BlockSpec(memory_space=pl.ANY),
            ],
            out_specs=pl.BlockSpec((tm, dp), lambda j, te, nt, sl, nx: (j, 0)),
            scratch_shapes=[
                pltpu.VMEM((2, d, ff), F32),
                pltpu.VMEM((2, d, ff), F32),
                pltpu.VMEM((2, ff, d), F32),
                pltpu.VMEM((d, ff), BF16),
                pltpu.VMEM((d, ff), BF16),
                pltpu.VMEM((ff, d), BF16),
                pltpu.SemaphoreType.DMA((2,)),
            ],
        ),
        out_shape=jax.ShapeDtypeStruct((n_rows, dp), F32),
        compiler_params=_cparams(("arbitrary",)),
        name="experts",
    )(tile_expert, n_tiles, tile_slot, tile_next, xs, w1, w3, w2)


def _combine_kernel(pos1_ref, pos2_ref, x1_ref, route_ref, ys_ref, nw_ref, o_ref, y1, y2, sem):
    i = pl.program_id(0)
    tm = x1_ref.shape[0]
    slot = i % 2

    def start_tile(tile, buf):
        for r in range(tm):
            for prio, (pos_ref, dst) in enumerate(((pos1_ref, y1), (pos2_ref, y2))):
                pltpu.make_async_copy(ys_ref.at[pl.ds(pos_ref[tile * tm + r], 1), :],
                                      dst.at[buf, pl.ds(r, 1), :], sem.at[buf]).start(priority=prio)

    @pl.when(i == 0)
    def _():
        start_tile(0, 0)

    @pl.when(i + 1 < pl.num_programs(0))
    def _():
        start_tile(i + 1, 1 - slot)

    for dst in (y1, y2):
        pltpu.make_async_copy(ys_ref.at[pl.ds(0, tm), :], dst.at[slot], sem.at[slot]).wait()

    route = route_ref[...]
    g1 = route[:, 4:5]
    g2 = route[:, 5:6]
    x = x1_ref[...] + (g1 * y1[slot] + g2 * y2[slot])
    o_ref[...] = x * lax.rsqrt(jnp.mean(x * x, axis=-1, keepdims=True) + NORM_EPS) * nw_ref[...]


def _combine(pos1, pos2, x1, route, ys, norm_w, tm):
    n, d = x1.shape
    return pl.pallas_call(
        _combine_kernel,
        grid_spec=pltpu.PrefetchScalarGridSpec(
            num_scalar_prefetch=2,
            grid=(n // tm,),
            in_specs=[
                pl.BlockSpec((tm, d), lambda i, p1, p2: (i, 0)),
                pl.BlockSpec((tm, LANES), lambda i, p1, p2: (i, 0)),
                pl.BlockSpec(memory_space=pl.ANY),
                pl.BlockSpec((1, d), lambda i, p1, p2: (0, 0)),
            ],
            out_specs=pl.BlockSpec((tm, d), lambda i, p1, p2: (i, 0)),
            scratch_shapes=[
                pltpu.VMEM((2, tm, ys.shape[1]), ys.dtype),
                pltpu.VMEM((2, tm, ys.shape[1]), ys.dtype),
                pltpu.SemaphoreType.DMA((2,)),
            ],
        ),
        out_shape=jax.ShapeDtypeStruct((n, d), F32),
        compiler_params=_cparams(("arbitrary",)),
        name="combine",
    )(pos1, pos2, x1, route, ys, norm_w)


def _pick(n, pref):
    t = min(n, pref)
    assert n % t == 0, (n, pref)
    return t


def _lookup(table, idx):
    k = jnp.arange(table.shape[0], dtype=jnp.int32)
    return jnp.sum(jnp.where(idx[:, None] == k[None, :], table[None, :], 0), axis=1)


def _lane_vec(v, offset=0):
    return jnp.zeros((1, LANES), F32).at[0, offset:offset + v.shape[0]].set(v.astype(F32))


def kernel(x, norm_mix_w, w_in, gdn_conv_w, gdn_a_log, gdn_dt_bias, gdn_norm_w, rg_conv_w, rg_conv_b, rg_w_rgate, rg_b_rgate, rg_w_igate, rg_b_igate, rg_lambda, rg_norm_w, w_out, norm_ffn_w, router_w_group, router_b_group, router_w_expert, router_b_expert, expert_w1, expert_w3, expert_w2, norm_final_w):
    bsz, seq, d = x.shape
    n = bsz * seq
    depth = w_in.shape[0]
    assert depth == 1
    n_heads = gdn_a_log.shape[1]
    gw = n_heads * HEAD_DIM
    rw = rg_lambda.shape[1]
    n_blocks = rg_w_rgate.shape[1]
    n_groups = router_w_group.shape[2]
    n_experts = expert_w1.shape[1]
    per_group = n_experts // n_groups
    assert gw == rw and rw == n_blocks * LANES
    assert n_groups + n_experts <= LANES and 2 * n_heads <= LANES

    x2 = x.reshape(n, d)
    wi = w_in[0]
    nb = 4 * gw
    w_main = jnp.concatenate([wi[:, :nb], wi[:, nb + 2 * n_heads:]], axis=1).astype(BF16)
    w_ba = jnp.pad(wi[:, nb:nb + 2 * n_heads], ((0, 0), (0, LANES - 2 * n_heads))).astype(BF16)
    alog_l = _lane_vec(gdn_a_log[0], n_heads)
    dtb_l = _lane_vec(gdn_dt_bias[0], n_heads)

    tm_in = _pick(n, 1024)
    proj, gb, gt = _in_proj(x2, norm_mix_w[0][None, :], w_main, w_ba, alog_l, dtb_l, tm_in, gw, n_heads)
    proj4 = proj.reshape(proj.shape[0], bsz, seq, gw)

    lc = _pick(seq, 4 * GDN_CHUNK)
    y_gdn = _gdn(proj4, gb, gt, 0.5 * gdn_conv_w[0], gdn_norm_w[0][None, :], bsz, seq, n_heads, lc)

    wri = jnp.concatenate([rg_w_rgate[0], rg_w_igate[0]], axis=-1).astype(BF16)
    lr = _pick(seq, 256)
    y_rg = _rglru(proj4, rg_conv_w[0], rg_conv_b[0][None, :], wri, rg_b_rgate[0][None, :],
                  rg_b_igate[0][None, :], rg_lambda[0][None, :], rg_norm_w[0][None, :],
                  bsz, seq, n_blocks, lr, 4, 5)

    wo = w_out[0].astype(BF16)
    w_router = jnp.pad(jnp.concatenate([router_w_group[0], router_w_expert[0]], axis=1),
                       ((0, 0), (0, LANES - n_groups - n_experts)))
    w_router_hi = w_router.astype(BF16)
    w_router_lo = (w_router - w_router_hi.astype(F32)).astype(BF16)
    w_router = jnp.concatenate([w_router_hi, w_router_lo], axis=1)
    b_router = _lane_vec(jnp.concatenate([router_b_group[0], router_b_expert[0]]))
    tm_o = _pick(n, 512)
    x1, h2, route, route_t, counts = _out_route(y_gdn.reshape(n, gw), y_rg.reshape(n, rw), x2, wo,
                                       norm_ffn_w[0][None, :], w_router, b_router, tm_o,
                                       n_groups, per_group)

    tm_e = 256
    cnt = counts[0, :n_experts].astype(jnp.int32)
    tiles_per = (cnt + tm_e - 1) // tm_e
    tile_end = jnp.cumsum(tiles_per)
    tile_start = tile_end - tiles_per
    max_tiles = (2 * n) // tm_e + n_experts
    n_rows = max_tiles * tm_e
    n_tiles = tile_end[-1:].astype(jnp.int32)
    jt = jnp.minimum(jnp.arange(max_tiles, dtype=jnp.int32), n_tiles[0] - 1)
    tile_expert = jnp.sum((tile_end[None, :] <= jt[:, None]).astype(jnp.int32), axis=1)
    row_start = tile_start * tm_e
    has_tiles = tiles_per > 0
    eidx = jnp.arange(n_experts, dtype=jnp.int32)
    later = jnp.where(has_tiles[None, :] & (eidx[None, :] > eidx[:, None]), eidx[None, :], n_experts)
    next_expert = jnp.min(later, axis=1)
    next_expert = jnp.where(next_expert < n_experts, next_expert, -1).astype(jnp.int32)
    expert_slot = ((jnp.cumsum(has_tiles.astype(jnp.int32)) - 1) % 2).astype(jnp.int32)
    tile_slot = _lookup(expert_slot, tile_expert)
    tile_next = _lookup(next_expert, tile_expert)
    tail = n_tiles[0] + jnp.arange(n_experts, dtype=jnp.int32)
    pad_row = jnp.concatenate([jnp.where(tiles_per > 0, (tile_end - 1) * tm_e, -1),
                               jnp.where(tail < max_tiles, tail * tm_e, -1)]).astype(jnp.int32)
    ri = route_t[:4].astype(jnp.int32)
    pos1 = _lookup(row_start, ri[0]) + ri[2]
    pos2 = _lookup(row_start, ri[1]) + ri[3]

    tm_d = _pick(n, 512)
    xs = _dispatch(pos1, pos2, pad_row, h2, n_rows, tm_d, tm_e)
    ys = _experts(tile_expert, n_tiles, tile_slot, tile_next, xs, expert_w1[0], expert_w3[0], expert_w2[0], tm_e)
    out = _combine(pos1, pos2, x1, route, ys, norm_final_w[None, :], tm_d)
    return out.reshape(bsz, seq, d)
```

```python
import functools

import jax
import jax.numpy as jnp
from jax import lax
from jax.experimental import pallas as pl
from jax.experimental.pallas import tpu as pltpu

F32 = jnp.float32
BF16 = jnp.bfloat16
NORM_EPS = 1e-6
LRU_C = 8.0
LANES = 128
SUBLANES = 8
GDN_CHUNK = 64
HEAD_DIM = 128
CONV_W = 4
HALO = 2 * SUBLANES
CONV_ROWS = 128
VMEM_LIMIT = 56 * 1024 * 1024
HI = lax.Precision.HIGHEST


def _cparams(sem):
    return pltpu.CompilerParams(dimension_semantics=sem, vmem_limit_bytes=VMEM_LIMIT)


def _softplus(x):
    return jnp.maximum(x, 0.0) + jnp.log1p(jnp.exp(-jnp.abs(x)))


def _sigmoid(x):
    return 0.5 * (jnp.tanh(0.5 * x) + 1.0)


def _shift_matrix():
    r = lax.broadcasted_iota(jnp.int32, ((CONV_W - 1) * CONV_ROWS, CONV_ROWS + HALO), 0)
    c = lax.broadcasted_iota(jnp.int32, ((CONV_W - 1) * CONV_ROWS, CONV_ROWS + HALO), 1)
    return (c == (r % CONV_ROWS) + (r // CONV_ROWS) + HALO - (CONV_W - 1)).astype(BF16)


def _causal_conv(shift, xwin, w_ref, lanes):
    sh = _dot(shift, xwin)
    acc = w_ref[CONV_W - 1:CONV_W, lanes] * xwin[HALO:, :].astype(F32)
    for j in range(CONV_W - 1):
        acc = acc + w_ref[j:j + 1, lanes] * sh[j * CONV_ROWS:(j + 1) * CONV_ROWS]
    return acc


def _silu_half(hx):
    return hx + hx * jnp.tanh(hx)


def _silu(x):
    return _silu_half(0.5 * x)


def _dot(a, b, precision=None):
    return jnp.dot(a, b, preferred_element_type=F32, precision=precision)


def _dot_nt(a, b, precision=None):
    return lax.dot_general(a, b, (((1,), (1,)), ((), ())), preferred_element_type=F32,
                           precision=precision)


def _regroup_kernel(a_ref, b_ref, o_ref, *, n_aligned, skip):
    j = pl.program_id(0)

    @pl.when(j < n_aligned)
    def _():
        o_ref[...] = a_ref[...].astype(o_ref.dtype)

    @pl.when(j >= n_aligned)
    def _():
        tn = a_ref.shape[1]
        win = jnp.concatenate([a_ref[...], b_ref[...]], axis=1)
        o_ref[...] = win[:, skip:skip + tn].astype(o_ref.dtype)


def _regroup_w_in(wi, tn, n_aligned, skip, n_pieces):
    d, cols = wi.shape
    assert n_aligned * tn + skip + (n_pieces - n_aligned) * tn == cols and skip <= LANES
    per = tn // LANES
    last_b = (cols - 1) // LANES
    return pl.pallas_call(
        functools.partial(_regroup_kernel, n_aligned=n_aligned, skip=skip),
        grid=(n_pieces,),
        in_specs=[
            pl.BlockSpec((d, tn), lambda j: (0, j)),
            pl.BlockSpec((d, LANES), lambda j: (0, jnp.minimum((j + 1) * per, last_b))),
        ],
        out_specs=pl.BlockSpec((d, tn), lambda j: (0, j)),
        out_shape=jax.ShapeDtypeStruct((d, n_pieces * tn), BF16),
        compiler_params=_cparams(("arbitrary",)),
        name="regroup_w_in",
    )(wi, wi)


def _in_proj_kernel(x_ref, nw_ref, w_ref, wba_ref, alog_ref, dtb_ref, o_ref, gb_ref, gt_ref, h_scr,
                    *, n_heads):
    j = pl.program_id(1)

    @pl.when(j == 0)
    def _():
        x = x_ref[...]
        ms = jnp.mean(x * x, axis=-1, keepdims=True)
        h = (x * lax.rsqrt(ms + NORM_EPS) * nw_ref[...]).astype(BF16)
        h_scr[...] = h
        ba = _dot(h, wba_ref[...])
        lane = lax.broadcasted_iota(jnp.int32, ba.shape, 1)
        beta = _sigmoid(ba)
        g = -jnp.exp(alog_ref[...]) * _softplus(ba + dtb_ref[...])
        gb = jnp.where(lane < n_heads, beta, g)
        gb_ref[...] = gb
        gt_ref[...] = gb.T[:2 * n_heads, :]

    o_ref[0] = _dot(h_scr[...], w_ref[...]).astype(o_ref.dtype)


def _in_proj(x2, norm_w, w_main, w_ba, alog_l, dtb_l, tm, tn, n_heads):
    n, d = x2.shape
    npieces = w_main.shape[1] // tn
    return pl.pallas_call(
        functools.partial(_in_proj_kernel, n_heads=n_heads),
        grid=(n // tm, npieces),
        in_specs=[
            pl.BlockSpec((tm, d), lambda i, j: (i, 0)),
            pl.BlockSpec((1, d), lambda i, j: (0, 0)),
            pl.BlockSpec((d, tn), lambda i, j: (0, j)),
            pl.BlockSpec((d, LANES), lambda i, j: (0, 0)),
            pl.BlockSpec((1, LANES), lambda i, j: (0, 0)),
            pl.BlockSpec((1, LANES), lambda i, j: (0, 0)),
        ],
        out_specs=[
            pl.BlockSpec((1, tm, tn), lambda i, j: (j, i, 0)),
            pl.BlockSpec((tm, LANES), lambda i, j: (i, 0)),
            pl.BlockSpec((2 * n_heads, tm), lambda i, j: (0, i)),
        ],
        out_shape=[
            jax.ShapeDtypeStruct((npieces, n, tn), BF16),
            jax.ShapeDtypeStruct((n, LANES), F32),
            jax.ShapeDtypeStruct((2 * n_heads, n), F32),
        ],
        scratch_shapes=[pltpu.VMEM((tm, d), BF16)],
        compiler_params=_cparams(("arbitrary", "arbitrary")),
        name="in_proj",
    )(x2, norm_w, w_main, w_ba, alog_l, dtb_l)


def _gdn_kernel(q_ref, k_ref, v_ref, z_ref, gb_ref, gt_ref, cw_ref, nw_ref, o_ref,
                xbuf, qkv, state, *, n_heads, lc):
    t = pl.program_id(1)
    hd = HEAD_DIM
    width = n_heads * hd
    c = GDN_CHUNK

    @pl.when(t == 0)
    def _():
        xbuf[0:HALO, :] = jnp.zeros((HALO, 3 * width), BF16)
        state[...] = jnp.zeros_like(state)

    xbuf[HALO:HALO + lc, 0:width] = q_ref[0, 0]
    xbuf[HALO:HALO + lc, width:2 * width] = k_ref[0, 0]
    xbuf[HALO:HALO + lc, 2 * width:3 * width] = v_ref[0, 0]

    row = lax.broadcasted_iota(jnp.int32, (c, c), 0)
    col = lax.broadcasted_iota(jnp.int32, (c, c), 1)
    causal = row >= col
    strict = row > col
    eye = (row == col).astype(F32)
    ltri = causal.astype(F32)
    r2 = lax.broadcasted_iota(jnp.int32, (lc, lc), 0)
    c2 = lax.broadcasted_iota(jnp.int32, (lc, lc), 1)
    utri = ((r2 <= c2) & ((r2 // c) == (c2 // c))).astype(F32)
    grow_all = _dot(gt_ref[...], utri, HI)

    def l2n(x, scale=None):
        inv = lax.rsqrt(jnp.sum(x * x, axis=-1, keepdims=True) + NORM_EPS)
        return x * (inv if scale is None else inv * scale)

    nci = lc // c
    gcols = [_dot(ltri, gb_ref[ci * c:(ci + 1) * c, :], HI) for ci in range(nci)]

    shift = _shift_matrix()
    pair = 2 * hd
    for piece in range(3):
        for hp in range(width // pair):
            lanes = slice(piece * width + hp * pair, piece * width + (hp + 1) * pair)
            for sub in range(lc // CONV_ROWS):
                rows = slice(sub * CONV_ROWS, (sub + 1) * CONV_ROWS)
                y = _silu_half(_causal_conv(shift, xbuf[sub * CONV_ROWS:(sub + 1) * CONV_ROWS + HALO, lanes],
                                            cw_ref, lanes))
                for half in range(2):
                    yh = y[:, half * hd:(half + 1) * hd]
                    if piece == 0:
                        yh = l2n(yh, hd ** -0.5)
                    elif piece == 1:
                        yh = l2n(yh)
                    qkv[rows, lanes.start + half * hd:lanes.start + (half + 1) * hd] = yh
    xbuf[0:HALO, :] = xbuf[lc:lc + HALO, :]

    blk = SUBLANES * 2
    diag_blk = (row // blk) == (col // blk)
    bf = lambda v: v.astype(BF16)
    heads = range(n_heads)

    def prepare(cis):
        units = [(ci, h) for ci in cis for h in heads]
        eg_cols = {ci: jnp.exp(gcols[ci]) for ci in cis}
        st = {}
        for (ci, h) in units:
            sl = slice(ci * c, (ci + 1) * c)
            qn = qkv[sl, h * hd:(h + 1) * hd]
            kn = qkv[sl, width + h * hd:width + (h + 1) * hd]
            vv = qkv[sl, 2 * width + h * hd:2 * width + (h + 1) * hd]
            beta_b = jnp.broadcast_to(gb_ref[sl, h:h + 1], (c, hd))
            gc_b = jnp.broadcast_to(gcols[ci][:, n_heads + h:n_heads + h + 1], (c, hd))
            eg = jnp.broadcast_to(eg_cols[ci][:, n_heads + h:n_heads + h + 1], (c, hd))
            gc_row = grow_all[n_heads + h:n_heads + h + 1, sl]
            qkk = _dot_nt(bf(jnp.concatenate([qn, kn], axis=0)), bf(kn))
            dmat = gc_b[:, :c] - gc_row
            decay = jnp.where(causal, jnp.exp(jnp.minimum(dmat, 0.0)), 0.0)
            a = jnp.where(strict, beta_b[:, :c] * qkk[c:] * decay, 0.0)
            dm = jnp.where(diag_blk, a, 0.0)
            st[(ci, h)] = dict(
                attn=bf(qkk[:c] * decay), dm=dm, lm=bf(a - dm),
                rhs=bf(jnp.concatenate([vv * beta_b, kn * (beta_b * eg)], axis=1)),
                qg=qn * eg,
                kdt=bf(kn.T * jnp.exp(gc_row[:, c - 1:c] - gc_row)),
                sdec=eg[c - 1:c, :])

        for u in units:
            d = st[u]
            d["x"] = _dot(bf(d["dm"]), bf(d["dm"]))
            d["p"] = eye - d["dm"]
        nsq = blk.bit_length() - 2
        for it in range(nsq):
            for u in units:
                d = st[u]
                if it < nsq - 1:
                    px = _dot(bf(jnp.concatenate([d["p"], d["x"]], axis=0)), bf(d["x"]))
                    d["p"] = d["p"] + px[:c]
                    d["x"] = px[c:]
                else:
                    d["p"] = d["p"] + _dot(bf(d["p"]), bf(d["x"]))
        for u in units:
            d = st[u]
            pb = bf(d["p"])
            d["m"] = _dot(pb, d["lm"])
            d["y"] = _dot(pb, d["rhs"])
        npow = c // blk - 1
        for u in units:
            d = st[u]
            d["tp"] = eye - d["m"]
            d["mk"] = d["m"]
        for k in range(2, npow + 1):
            for u in units:
                d = st[u]
                d["mk"] = _dot(bf(d["m"]), bf(d["mk"]))
                d["tp"] = d["tp"] + d["mk"] if k % 2 == 0 else d["tp"] - d["mk"]
        for u in units:
            d = st[u]
            d["sol"] = _dot(bf(d["tp"]), bf(d["y"]))
        return st

    states = [state[h] for h in heads]
    st = prepare(range(nci))
    for ci in range(nci):
        sl = slice(ci * c, (ci + 1) * c)
        ws = {}
        for h in heads:
            d = st[(ci, h)]
            wq = bf(jnp.concatenate([d["sol"][:, hd:], d["qg"]], axis=0))
            ws[h] = _dot(wq, bf(states[h]))
        upd = {}
        vnew = {}
        for h in heads:
            d = st[(ci, h)]
            vnew[h] = bf(d["sol"][:, :hd] - ws[h][:c])
            upd[h] = _dot(jnp.concatenate([d["attn"], d["kdt"]], axis=0), vnew[h])
        for h in heads:
            d = st[(ci, h)]
            o = ws[h][c:] + upd[h][:c]
            states[h] = states[h] * d["sdec"] + upd[h][c:]
            on = o * lax.rsqrt(jnp.mean(o * o, axis=-1, keepdims=True) + NORM_EPS) * nw_ref[...]
            zz = z_ref[0, 0, sl, h * hd:(h + 1) * hd].astype(F32)
            o_ref[0, sl, h * hd:(h + 1) * hd] = (on * _silu(zz)).astype(o_ref.dtype)
    for h in heads:
        state[h] = states[h]


def _gdn(proj4, gb, gt, conv_w, norm_w, bsz, seq, n_heads, lc):
    width = n_heads * HEAD_DIM
    nt = seq // lc
    kern = functools.partial(_gdn_kernel, n_heads=n_heads, lc=lc)

    def piece(p):
        return pl.BlockSpec((1, 1, lc, width), lambda b, t, p=p: (p, b, t, 0))

    return pl.pallas_call(
        kern,
        grid=(bsz, nt),
        in_specs=[
            piece(0), piece(1), piece(2), piece(3),
            pl.BlockSpec((lc, LANES), lambda b, t: (b * nt + t, 0)),
            pl.BlockSpec((2 * n_heads, lc), lambda b, t: (0, b * nt + t)),
            pl.BlockSpec((CONV_W, 3 * width), lambda b, t: (0, 0)),
            pl.BlockSpec((1, HEAD_DIM), lambda b, t: (0, 0)),
        ],
        out_specs=pl.BlockSpec((1, lc, width), lambda b, t: (b, t, 0)),
        out_shape=jax.ShapeDtypeStruct((bsz, seq, width), BF16),
        scratch_shapes=[
            pltpu.VMEM((lc + HALO, 3 * width), BF16),
            pltpu.VMEM((lc, 3 * width), F32),
            pltpu.VMEM((n_heads, HEAD_DIM, HEAD_DIM), F32),
        ],
        compiler_params=_cparams(("arbitrary", "arbitrary")),
        name="gdn",
    )(proj4, proj4, proj4, proj4, gb, gt, conv_w, norm_w)


def _rglru_kernel(x_ref, g_ref, cw_ref, cb_ref, wri_ref, br_ref, bi_ref, lam_ref, nw_ref, o_ref,
                  xbuf, a_scr, b_scr, hlast, *, n_blocks, lr):
    t = pl.program_id(1)
    bd = LANES
    width = n_blocks * bd

    @pl.when(t == 0)
    def _():
        xbuf[0:HALO, :] = jnp.zeros((HALO, width), BF16)
        hlast[...] = jnp.zeros_like(hlast)

    xbuf[HALO:HALO + lr, :] = x_ref[0, 0]
    shift = _shift_matrix()
    pair = 2 * bd
    xc_pairs = []
    for p in range(width // pair):
        lanes = slice(p * pair, (p + 1) * pair)
        xc_pairs.append(jnp.concatenate(
            [_causal_conv(shift, xbuf[s0:s0 + CONV_ROWS + HALO, lanes], cw_ref, lanes)
             for s0 in range(0, lr, CONV_ROWS)], axis=0) + cb_ref[:, lanes])
    xbuf[0:HALO, :] = xbuf[lr:lr + HALO, :]

    sp_lam = _softplus(-lam_ref[...])
    sub = lax.broadcasted_iota(jnp.int32, (lr, bd), 0) % SUBLANES
    for n in range(n_blocks):
        sl = slice(n * bd, (n + 1) * bd)
        xb = xc_pairs[n // 2][:, (n % 2) * bd:(n % 2 + 1) * bd]
        gates = _dot(xb.astype(BF16), wri_ref[n])
        r = _sigmoid(gates[:, :bd] + br_ref[:, sl])
        i = _sigmoid(gates[:, bd:] + bi_ref[:, sl])
        log_a = -LRU_C * r * sp_lam[:, sl]
        a = jnp.exp(log_a)
        th = jnp.tanh(log_a)
        b = jnp.sqrt(-2.0 * th / (1.0 - th)) * (i * xb)
        for s in (1, 2, 4):
            a_s = pltpu.roll(a, s, axis=0)
            b_s = pltpu.roll(b, s, axis=0)
            m = sub >= s
            b = jnp.where(m, a * b_s + b, b)
            a = jnp.where(m, a * a_s, a)
        a_scr[:, sl] = a
        b_scr[:, sl] = b

    def tile_step(i, hprev):
        r0 = pl.multiple_of(i * SUBLANES, SUBLANES)
        hh = a_scr[pl.ds(r0, SUBLANES), :] * hprev + b_scr[pl.ds(r0, SUBLANES), :]
        b_scr[pl.ds(r0, SUBLANES), :] = hh
        return jnp.broadcast_to(hh[SUBLANES - 1:SUBLANES, :], (SUBLANES, width))

    hfin = lax.fori_loop(0, lr // SUBLANES, tile_step, hlast[...])
    hlast[...] = hfin

    for n in range(n_blocks):
        sl = slice(n * bd, (n + 1) * bd)
        gt = g_ref[0, 0, :, sl].astype(F32)
        y = b_scr[:, sl] * jax.nn.gelu(gt, approximate=True)
        yn = y * lax.rsqrt(jnp.mean(y * y, axis=-1, keepdims=True) + NORM_EPS) * nw_ref[:, sl]
        o_ref[0, :, sl] = yn.astype(o_ref.dtype)


def _rglru(proj4, conv_w, conv_b, wri, br, bi, lam, norm_w, bsz, seq, n_blocks, lr, px, pg):
    width = n_blocks * LANES
    nt = seq // lr
    kern = functools.partial(_rglru_kernel, n_blocks=n_blocks, lr=lr)

    def vec():
        return pl.BlockSpec((1, width), lambda b, t: (0, 0))

    return pl.pallas_call(
        kern,
        grid=(bsz, nt),
        in_specs=[
            pl.BlockSpec((1, 1, lr, width), lambda b, t: (px, b, t, 0)),
            pl.BlockSpec((1, 1, lr, width), lambda b, t: (pg, b, t, 0)),
            pl.BlockSpec((CONV_W, width), lambda b, t: (0, 0)),
            vec(),
            pl.BlockSpec((n_blocks, LANES, 2 * LANES), lambda b, t: (0, 0, 0)),
            vec(), vec(), vec(), vec(),
        ],
        out_specs=pl.BlockSpec((1, lr, width), lambda b, t: (b, t, 0)),
        out_shape=jax.ShapeDtypeStruct((bsz, seq, width), BF16),
        scratch_shapes=[
            pltpu.VMEM((lr + HALO, width), BF16),
            pltpu.VMEM((lr, width), F32),
            pltpu.VMEM((lr, width), F32),
            pltpu.VMEM((SUBLANES, width), F32),
        ],
        compiler_params=_cparams(("arbitrary", "arbitrary")),
        name="rglru",
    )(proj4, proj4, conv_w, conv_b, wri, br, bi, lam, norm_w)


def _out_route_kernel(yg_ref, yr_ref, x_ref, wog_ref, wor_ref, nw_ref, wr_ref, rb_ref,
                      x1_ref, h2_ref, route_ref, routet_ref, cnt_ref, cnt_scr, *, n_groups, per_group):
    i = pl.program_id(0)
    ts = x_ref.shape[0]

    @pl.when(i == 0)
    def _():
        cnt_scr[...] = jnp.zeros_like(cnt_scr)

    lane = lax.broadcasted_iota(jnp.int32, (ts, LANES), 1)
    neg = jnp.float32(-jnp.inf)
    big = jnp.int32(1 << 20)
    r = lax.broadcasted_iota(jnp.int32, (ts, ts), 0)
    c = lax.broadcasted_iota(jnp.int32, (ts, ts), 1)
    before = (r > c).astype(BF16)

    def first_argmax(vals, mask):
        mx = jnp.max(jnp.where(mask, vals, neg), axis=-1, keepdims=True)
        idx = jnp.min(jnp.where(mask & (vals == mx), lane, big), axis=-1, keepdims=True)
        return mx, idx

    x1 = x_ref[...] + _dot(yg_ref[...], wog_ref[...]) + _dot(yr_ref[...], wor_ref[...])
    x1_ref[...] = x1
    h2 = x1 * lax.rsqrt(jnp.mean(x1 * x1, axis=-1, keepdims=True) + NORM_EPS) * nw_ref[...]
    h2_ref[...] = h2
    h_hi = h2.astype(BF16)
    h_lo = (h2 - h_hi.astype(F32)).astype(BF16)
    hw = _dot(h_hi, wr_ref[...])
    logits = hw[:, :LANES] + hw[:, LANES:] + _dot(h_lo, wr_ref[:, :LANES]) + rb_ref[...]
    counts = cnt_scr[...]
    gmask = lane < n_groups
    gmax, gidx = first_argmax(logits, gmask)
    zg = jnp.sum(jnp.where(gmask, jnp.exp(logits - gmax), 0.0), axis=-1, keepdims=True)
    grp_p = 1.0 / zg
    lo = n_groups + gidx * per_group
    emask = (lane >= lo) & (lane < lo + per_group)
    m1, i1 = first_argmax(logits, emask)
    m2, i2 = first_argmax(logits, emask & (lane != i1))
    ze = jnp.sum(jnp.where(emask, jnp.exp(logits - m1), 0.0), axis=-1, keepdims=True)
    p1 = 1.0 / ze
    p2 = jnp.exp(m2 - m1) / ze
    den = p1 + p2
    g1 = grp_p * (p1 / den)
    g2 = grp_p * (p2 / den)
    e1 = i1 - n_groups
    e2 = i2 - n_groups
    oh1 = lane == e1
    oh2 = lane == e2
    onehot = jnp.where(oh1 | oh2, 1.0, 0.0)
    prefix = _dot(before, onehot.astype(BF16)) + counts
    rank1 = jnp.sum(jnp.where(oh1, prefix, 0.0), axis=-1, keepdims=True)
    rank2 = jnp.sum(jnp.where(oh2, prefix, 0.0), axis=-1, keepdims=True)
    counts = counts + jnp.sum(onehot, axis=0, keepdims=True)
    out = jnp.where(lane == 0, e1.astype(F32), 0.0)
    out = jnp.where(lane == 1, e2.astype(F32), out)
    out = jnp.where(lane == 2, rank1, out)
    out = jnp.where(lane == 3, rank2, out)
    out = jnp.where(lane == 4, g1, out)
    out = jnp.where(lane == 5, g2, out)
    route_ref[...] = out
    routet_ref[...] = out.T[:SUBLANES, :]
    cnt_scr[...] = counts
    cnt_ref[...] = counts


def _out_route(y_gdn, y_rg, x2, wo, norm_w, w_router, b_router, tm, n_groups, per_group):
    n, d = x2.shape
    wg = y_gdn.shape[1]
    wr = y_rg.shape[1]
    assert wg == wr and wo.shape[0] == wg + wr
    kern = functools.partial(_out_route_kernel, n_groups=n_groups, per_group=per_group)
    return pl.pallas_call(
        kern,
        grid=(n // tm,),
        in_specs=[
            pl.BlockSpec((tm, wg), lambda i: (i, 0)),
            pl.BlockSpec((tm, wr), lambda i: (i, 0)),
            pl.BlockSpec((tm, d), lambda i: (i, 0)),
            pl.BlockSpec((wg, d), lambda i: (0, 0), pipeline_mode=pl.Buffered(1)),
            pl.BlockSpec((wr, d), lambda i: (1, 0), pipeline_mode=pl.Buffered(1)),
            pl.BlockSpec((1, d), lambda i: (0, 0)),
            pl.BlockSpec((d, 2 * LANES), lambda i: (0, 0), pipeline_mode=pl.Buffered(1)),
            pl.BlockSpec((1, LANES), lambda i: (0, 0)),
        ],
        out_specs=[
            pl.BlockSpec((tm, d), lambda i: (i, 0)),
            pl.BlockSpec((tm, d), lambda i: (i, 0)),
            pl.BlockSpec((tm, LANES), lambda i: (i, 0)),
            pl.BlockSpec((SUBLANES, tm), lambda i: (0, i)),
            pl.BlockSpec((1, LANES), lambda i: (0, 0)),
        ],
        out_shape=[
            jax.ShapeDtypeStruct((n, d), F32),
            jax.ShapeDtypeStruct((n, d), F32),
            jax.ShapeDtypeStruct((n, LANES), F32),
            jax.ShapeDtypeStruct((SUBLANES, n), F32),
            jax.ShapeDtypeStruct((1, LANES), F32),
        ],
        scratch_shapes=[pltpu.VMEM((1, LANES), F32)],
        compiler_params=_cparams(("arbitrary",)),
        name="out_route",
    )(y_gdn, y_rg, x2, wo, wo, norm_w, w_router, b_router)


def _dispatch_kernel(pos1_ref, pos2_ref, pad_ref, h_ref, xs_ref, zeros, sem, zsem, *, n_pad):
    i = pl.program_id(0)
    tm = h_ref.shape[0]
    tz = zeros.shape[0]

    @pl.when(i == 0)
    def _():
        zeros[...] = jnp.zeros_like(zeros)

        def zero_copy(e):
            row = pl.multiple_of(jnp.maximum(pad_ref[e], 0), SUBLANES)
            return pltpu.make_async_copy(zeros, xs_ref.at[pl.ds(row, tz), :], zsem)

        for e in range(n_pad):
            @pl.when(pad_ref[e] >= 0)
            def _():
                zero_copy(e).start()
        for e in range(n_pad):
            @pl.when(pad_ref[e] >= 0)
            def _():
                zero_copy(e).wait()

    def row_copy(r, pos_ref):
        return pltpu.make_async_copy(h_ref.at[pl.ds(r, 1), :],
                                     xs_ref.at[pl.ds(pos_ref[i * tm + r], 1), :], sem)

    for r in range(tm):
        row_copy(r, pos1_ref).start(priority=0)
        row_copy(r, pos2_ref).start(priority=1)
    for _ in range(2):
        pltpu.make_async_copy(h_ref, xs_ref.at[pl.ds(0, tm), :], sem).wait()


def _dispatch(pos1, pos2, pad_row, h2, n_rows, tm, tm_e):
    n, d = h2.shape
    return pl.pallas_call(
        functools.partial(_dispatch_kernel, n_pad=pad_row.shape[0]),
        grid_spec=pltpu.PrefetchScalarGridSpec(
            num_scalar_prefetch=3,
            grid=(n // tm,),
            in_specs=[pl.BlockSpec((tm, d), lambda i, p1, p2, pr: (i, 0))],
            out_specs=pl.BlockSpec(memory_space=pl.ANY),
            scratch_shapes=[pltpu.VMEM((tm_e, d), h2.dtype),
                            pltpu.SemaphoreType.DMA(()), pltpu.SemaphoreType.DMA(())],
        ),
        out_shape=jax.ShapeDtypeStruct((n_rows, d), h2.dtype),
        compiler_params=pltpu.CompilerParams(dimension_semantics=("arbitrary",),
                                             vmem_limit_bytes=VMEM_LIMIT, has_side_effects=True),
        name="dispatch",
    )(pos1, pos2, pad_row, h2)


def _experts_kernel(te_ref, nt_ref, slot_ref, next_ref, xs_ref, w1_ref, w3_ref, w2_ref, ys_ref,
                    w1f, w3f, w2f, w1b, w3b, w2b, sem):
    j = pl.program_id(0)
    prev = te_ref[jnp.maximum(j - 1, 0)]
    active = j < nt_ref[0]

    def weight_copies(e, buf):
        return [pltpu.make_async_copy(src.at[e], dst.at[buf], sem.at[buf])
                for src, dst in ((w1_ref, w1f), (w3_ref, w3f), (w2_ref, w2f))]

    @pl.when(active & ((j == 0) | (te_ref[j] != prev)))
    def _():
        buf = slot_ref[j]

        @pl.when(j == 0)
        def _():
            for cp in weight_copies(te_ref[0], 0):
                cp.start()

        for cp in weight_copies(te_ref[j], buf):
            cp.wait()

        @pl.when(next_ref[j] >= 0)
        def _():
            for cp in weight_copies(next_ref[j], 1 - buf):
                cp.start()

        w1b[...] = w1f[buf].astype(BF16)
        w3b[...] = w3f[buf].astype(BF16)
        w2b[...] = w2f[buf].astype(BF16)

    @pl.when(active)
    def _():
        x = xs_ref[...].astype(BF16)
        h1 = _dot(x, w1b[...])
        h3 = _dot(x, w3b[...])
        he = (_silu(h1) * h3).astype(BF16)
        ys_ref[...] = _dot(he, w2b[...])

    @pl.when(jnp.logical_not(active))
    def _():
        ys_ref[...] = jnp.zeros_like(ys_ref)


def _experts(tile_expert, n_tiles, tile_slot, tile_next, xs, w1, w3, w2, tm):
    n_rows, dp = xs.shape
    d, ff = w1.shape[1], w1.shape[2]

    def rowmap(j, te, nt, sl, nx):
        return (jnp.minimum(j, nt[0] - 1), 0)

    return pl.pallas_call(
        _experts_kernel,
        grid_spec=pltpu.PrefetchScalarGridSpec(
            num_scalar_prefetch=4,
            grid=(n_rows // tm,),
            in_specs=[
                pl.BlockSpec((tm, dp), rowmap),
                pl.BlockSpec(memory_space=pl.ANY),
                pl.BlockSpec(memory_space=pl.ANY),
                pl.BlockSpec(memory_space=pl.ANY),
            ],
            out_specs=pl.BlockSpec((tm, dp), lambda j, te, nt, sl, nx: (j, 0)),
            scratch_shapes=[
                pltpu.VMEM((2, d, ff), F32),
                pltpu.VMEM((2, d, ff), F32),
                pltpu.VMEM((2, ff, d), F32),
                pltpu.VMEM((d, ff), BF16),
                pltpu.VMEM((d, ff), BF16),
                pltpu.VMEM((ff, d), BF16),
                pltpu.SemaphoreType.DMA((2,)),
            ],
        ),
        out_shape=jax.ShapeDtypeStruct((n_rows, dp), F32),
        compiler_params=_cparams(("arbitrary",)),
        name="experts",
    )(tile_expert, n_tiles, tile_slot, tile_next, xs, w1, w3, w2)


def _combine_kernel(pos1_ref, pos2_ref, x1_ref, route_ref, ys_ref, nw_ref, o_ref, y1, y2, sem):
    i = pl.program_id(0)
    tm = x1_ref.shape[0]
    slot = i % 2

    def start_tile(tile, buf):
        for r in range(tm):
            for prio, (pos_ref, dst) in enumerate(((pos1_ref, y1), (pos2_ref, y2))):
                pltpu.make_async_copy(ys_ref.at[pl.ds(pos_ref[tile * tm + r], 1), :],
                                      dst.at[buf, pl.ds(r, 1), :], sem.at[buf]).start(priority=prio)

    @pl.when(i == 0)
    def _():
        start_tile(0, 0)

    @pl.when(i + 1 < pl.num_programs(0))
    def _():
        start_tile(i + 1, 1 - slot)

    for dst in (y1, y2):
        pltpu.make_async_copy(ys_ref.at[pl.ds(0, tm), :], dst.at[slot], sem.at[slot]).wait()

    route = route_ref[...]
    g1 = route[:, 4:5]
    g2 = route[:, 5:6]
    x = x1_ref[...] + (g1 * y1[slot] + g2 * y2[slot])
    o_ref[...] = x * lax.rsqrt(jnp.mean(x * x, axis=-1, keepdims=True) + NORM_EPS) * nw_ref[...]


def _combine(pos1, pos2, x1, route, ys, norm_w, tm):
    n, d = x1.shape
    return pl.pallas_call(
        _combine_kernel,
        grid_spec=pltpu.PrefetchScalarGridSpec(
            num_scalar_prefetch=2,
            grid=(n // tm,),
            in_specs=[
                pl.BlockSpec((tm, d), lambda i, p1, p2: (i, 0)),
                pl.BlockSpec((tm, LANES), lambda i, p1, p2: (i, 0)),
                pl.BlockSpec(memory_space=pl.ANY),
                pl.BlockSpec((1, d), lambda i, p1, p2: (0, 0)),
            ],
            out_specs=pl.BlockSpec((tm, d), lambda i, p1, p2: (i, 0)),
            scratch_shapes=[
                pltpu.VMEM((2, tm, ys.shape[1]), ys.dtype),
                pltpu.VMEM((2, tm, ys.shape[1]), ys.dtype),
                pltpu.SemaphoreType.DMA((2,)),
            ],
        ),
        out_shape=jax.ShapeDtypeStruct((n, d), F32),
        compiler_params=_cparams(("arbitrary",)),
        name="combine",
    )(pos1, pos2, x1, route, ys, norm_w)


def _pick(n, pref):
    t = min(n, pref)
    assert n % t == 0, (n, pref)
    return t


def _lookup(table, idx):
    k = jnp.arange(table.shape[0], dtype=jnp.int32)
    return jnp.sum(jnp.where(idx[:, None] == k[None, :], table[None, :], 0), axis=1)


def _lane_vec(v, offset=0):
    return jnp.zeros((1, LANES), F32).at[0, offset:offset + v.shape[0]].set(v.astype(F32))


def kernel(x, norm_mix_w, w_in, gdn_conv_w, gdn_a_log, gdn_dt_bias, gdn_norm_w, rg_conv_w, rg_conv_b, rg_w_rgate, rg_b_rgate, rg_w_igate, rg_b_igate, rg_lambda, rg_norm_w, w_out, norm_ffn_w, router_w_group, router_b_group, router_w_expert, router_b_expert, expert_w1, expert_w3, expert_w2, norm_final_w):
    bsz, seq, d = x.shape
    n = bsz * seq
    depth = w_in.shape[0]
    assert depth == 1
    n_heads = gdn_a_log.shape[1]
    gw = n_heads * HEAD_DIM
    rw = rg_lambda.shape[1]
    n_blocks = rg_w_rgate.shape[1]
    n_groups = router_w_group.shape[2]
    n_experts = expert_w1.shape[1]
    per_group = n_experts // n_groups
    assert gw == rw and rw == n_blocks * LANES
    assert n_groups + n_experts <= LANES and 2 * n_heads <= LANES

    x2 = x.reshape(n, d)
    wi = w_in[0]
    nb = 4 * gw
    w_main = _regroup_w_in(wi, gw, 4, 2 * n_heads, 6)
    w_ba = jnp.pad(wi[:, nb:nb + 2 * n_heads], ((0, 0), (0, LANES - 2 * n_heads))).astype(BF16)
    alog_l = _lane_vec(gdn_a_log[0], n_heads)
    dtb_l = _lane_vec(gdn_dt_bias[0], n_heads)

    tm_in = _pick(n, 1024)
    proj, gb, gt = _in_proj(x2, norm_mix_w[0][None, :], w_main, w_ba, alog_l, dtb_l, tm_in, gw, n_heads)
    proj4 = proj.reshape(proj.shape[0], bsz, seq, gw)

    lc = _pick(seq, 4 * GDN_CHUNK)
    y_gdn = _gdn(proj4, gb, gt, 0.5 * gdn_conv_w[0], gdn_norm_w[0][None, :], bsz, seq, n_heads, lc)

    wri = jnp.concatenate([rg_w_rgate[0], rg_w_igate[0]], axis=-1).astype(BF16)
    lr = _pick(seq, 512)
    y_rg = _rglru(proj4, rg_conv_w[0], rg_conv_b[0][None, :], wri, rg_b_rgate[0][None, :],
                  rg_b_igate[0][None, :], rg_lambda[0][None, :], rg_norm_w[0][None, :],
                  bsz, seq, n_blocks, lr, 4, 5)

    wo = w_out[0].astype(BF16)
    w_router = jnp.pad(jnp.concatenate([router_w_group[0], router_w_expert[0]], axis=1),
                       ((0, 0), (0, LANES - n_groups - n_experts)))
    w_router_hi = w_router.astype(BF16)
    w_router_lo = (w_router - w_router_hi.astype(F32)).astype(BF16)
    w_router = jnp.concatenate([w_router_hi, w_router_lo], axis=1)
    b_router = _lane_vec(jnp.concatenate([router_b_group[0], router_b_expert[0]]))
    tm_o = _pick(n, 512)
    x1, h2, route, route_t, counts = _out_route(y_gdn.reshape(n, gw), y_rg.reshape(n, rw), x2, wo,
                                       norm_ffn_w[0][None, :], w_router, b_router, tm_o,
                                       n_groups, per_group)

    tm_e = 256
    cnt = counts[0, :n_experts].astype(jnp.int32)
    tiles_per = (cnt + tm_e - 1) // tm_e
    tile_end = jnp.cumsum(tiles_per)
    tile_start = tile_end - tiles_per
    max_tiles = (2 * n) // tm_e + n_experts
    n_rows = max_tiles * tm_e
    n_tiles = tile_end[-1:].astype(jnp.int32)
    jt = jnp.minimum(jnp.arange(max_tiles, dtype=jnp.int32), n_tiles[0] - 1)
    tile_expert = jnp.sum((tile_end[None, :] <= jt[:, None]).astype(jnp.int32), axis=1)
    row_start = tile_start * tm_e
    has_tiles = tiles_per > 0
    eidx = jnp.arange(n_experts, dtype=jnp.int32)
    later = jnp.where(has_tiles[None, :] & (eidx[None, :] > eidx[:, None]), eidx[None, :], n_experts)
    next_expert = jnp.min(later, axis=1)
    next_expert = jnp.where(next_expert < n_experts, next_expert, -1).astype(jnp.int32)
    expert_slot = ((jnp.cumsum(has_tiles.astype(jnp.int32)) - 1) % 2).astype(jnp.int32)
    tile_slot = _lookup(expert_slot, tile_expert)
    tile_next = _lookup(next_expert, tile_expert)
    tail = n_tiles[0] + jnp.arange(n_experts, dtype=jnp.int32)
    pad_row = jnp.concatenate([jnp.where(tiles_per > 0, (tile_end - 1) * tm_e, -1),
                               jnp.where(tail < max_tiles, tail * tm_e, -1)]).astype(jnp.int32)
    ri = route_t[:4].astype(jnp.int32)
    pos1 = _lookup(row_start, ri[0]) + ri[2]
    pos2 = _lookup(row_start, ri[1]) + ri[3]

    tm_d = _pick(n, 512)
    xs = _dispatch(pos1, pos2, pad_row, h2, n_rows, tm_d, tm_e)
    ys = _experts(tile_expert, n_tiles, tile_slot, tile_next, xs, expert_w1[0], expert_w3[0], expert_w2[0], tm_e)
    out = _combine(pos1, pos2, x1, route, ys, norm_final_w[None, :], tm_d)
    return out.reshape(bsz, seq, d)
```
